```python
import jax, jax.numpy as jnp
from jax import lax
import numpy as np

D_MODEL = 1024
BATCH = 16
SEQ = 2048
DEPTH = 1

RMS_EPS = 1e-6
SSD_D_INNER = 2 * D_MODEL
SSD_HEADDIM = 64
SSD_N_HEADS = SSD_D_INNER // SSD_HEADDIM
SSD_N_GROUPS = 4
SSD_D_STATE = 128
SSD_CONV = 4
SSD_CHUNK = 128
SSD_CONV_DIM = SSD_D_INNER + 2 * SSD_N_GROUPS * SSD_D_STATE
MLSTM_N_HEADS = 4
MLSTM_D_V = 2 * D_MODEL
MLSTM_D_QK = D_MODEL
MLSTM_HEAD_V = MLSTM_D_V // MLSTM_N_HEADS
MLSTM_HEAD_QK = MLSTM_D_QK // MLSTM_N_HEADS
MLSTM_CONV = 4
MLSTM_CHUNK = 128
D_FF = -(-(8 * D_MODEL) // (3 * 256)) * 256
IN_SPLITS = (SSD_D_INNER,
             SSD_CONV_DIM,
             SSD_N_HEADS,
             2 * MLSTM_D_QK,
             MLSTM_D_V,
             MLSTM_D_V,
             2 * MLSTM_N_HEADS,
             2 * D_MODEL)
D_IN_PROJ = sum(IN_SPLITS)

kernel_name = "hybrid_ssd_mlstm_gated_adaln_block"


def split_sizes(x, sizes):
    return jnp.split(x, np.cumsum(sizes)[:-1].tolist(), axis=-1)


def rms_norm(x, eps=RMS_EPS):
    xf = x.astype(jnp.float32)
    return (xf * lax.rsqrt(jnp.mean(xf * xf, axis=-1, keepdims=True) + eps)).astype(x.dtype)


def causal_depthwise_conv(x, w, b):
    k = w.shape[0]
    y = lax.conv_general_dilated(x, w.astype(x.dtype)[:, None, :], window_strides=(1,),
                                 padding=[(k - 1, 0)], dimension_numbers=('NWC', 'WIO', 'NWC'),
                                 feature_group_count=x.shape[-1])
    return y + b


def segsum(a):
    t = a.shape[-1]
    cs = jnp.cumsum(a, axis=-1)
    diff = cs[..., :, None] - cs[..., None, :]
    return jnp.where(jnp.tril(jnp.ones((t, t), dtype=bool)), diff, -jnp.inf)


def ssd_chunked(x, a, bm, cm):
    bt, s, h, p = x.shape
    g, n = bm.shape[-2:]
    r = h // g
    L = SSD_CHUNK
    nc = s // L
    x = x.reshape(bt, nc, L, g, r, p)
    bm = bm.reshape(bt, nc, L, g, n)
    cm = cm.reshape(bt, nc, L, g, n)
    a = a.astype(jnp.float32).reshape(bt, nc, L, g, r).transpose(0, 3, 4, 1, 2)
    a_cs = jnp.cumsum(a, axis=-1)
    decay = jnp.exp(segsum(a))
    cb = jnp.einsum('bclgn,bcsgn->bgcls', cm, bm)
    y_diag = jnp.einsum('bgcls,bgrcls,bcsgrp->bclgrp', cb, decay, x)
    decay_states = jnp.exp(a_cs[..., -1:] - a_cs)
    states = jnp.einsum('bclgn,bgrcl,bclgrp->bcgrpn', bm, decay_states, x)
    states = jnp.concatenate([jnp.zeros_like(states[:, :1]), states], axis=1)
    chunk_a = jnp.pad(a_cs[..., -1], ((0, 0), (0, 0), (0, 0), (1, 0)))
    chunk_decay = jnp.exp(segsum(chunk_a))
    states = jnp.einsum('bgrzc,bcgrpn->bzgrpn', chunk_decay, states)[:, :-1]
    y_off = jnp.einsum('bclgn,bcgrpn,bgrcl->bclgrp', cm, states, jnp.exp(a_cs))
    return (y_diag + y_off).reshape(bt, s, h * p)


def ssd_branch(z, xbc, dt, conv_w, conv_b, dt_bias, a_log, d_skip, norm_w):
    bt, s, _ = z.shape
    xbc = jax.nn.silu(causal_depthwise_conv(xbc, conv_w, conv_b))
    xs, bm, cm = split_sizes(xbc, (SSD_D_INNER, SSD_N_GROUPS * SSD_D_STATE, SSD_N_GROUPS * SSD_D_STATE))
    xs = xs.reshape(bt, s, SSD_N_HEADS, SSD_HEADDIM)
    bm = bm.reshape(bt, s, SSD_N_GROUPS, SSD_D_STATE)
    cm = cm.reshape(bt, s, SSD_N_GROUPS, SSD_D_STATE)
    dt = jax.nn.softplus(dt.astype(jnp.float32) + dt_bias)
    a_cont = -jnp.exp(a_log.astype(jnp.float32))
    y = ssd_chunked(xs * dt[..., None], dt * a_cont, bm, cm)
    y = y + (xs * d_skip[:, None]).reshape(bt, s, SSD_D_INNER)
    return rms_norm(y * jax.nn.silu(z)) * norm_w


def mlstm_chunked(q, k, v, i_pre, f_pre):
    bt, s, h, dk = q.shape
    dv = v.shape[-1]
    L = MLSTM_CHUNK
    nc = s // L
    q = q.reshape(bt, nc, L, h, dk) * (dk ** -0.5)
    k = k.reshape(bt, nc, L, h, dk)
    v = v.reshape(bt, nc, L, h, dv)
    log_i = i_pre.astype(jnp.float32).reshape(bt, nc, L, h).transpose(0, 3, 1, 2)
    log_f = jax.nn.log_sigmoid(f_pre.astype(jnp.float32)).reshape(bt, nc, L, h).transpose(0, 3, 1, 2)
    b = jnp.cumsum(log_f, axis=-1)
    g = b[..., -1]
    a = g[..., None] - b + log_i
    m_loc = jnp.max(a, axis=-1)
    w = jnp.exp(a - m_loc[..., None])
    c_loc = jnp.einsum('bhcl,bclhk,bclhv->bchkv', w, k, v)
    n_loc = jnp.einsum('bhcl,bclhk->bchk', w, k)

    def step(carry, inp):
        c_prev, n_prev, m_prev = carry
        c_c, n_c, m_c, g_c = inp
        m_new = jnp.maximum(g_c + m_prev, m_c)
        sp = jnp.exp(g_c + m_prev - m_new)
        sc = jnp.exp(m_c - m_new)
        c_new = sp[..., None, None] * c_prev + sc[..., None, None] * c_c
        n_new = sp[..., None] * n_prev + sc[..., None] * n_c
        return (c_new, n_new, m_new), (c_prev, n_prev, m_prev)

    init = (jnp.zeros((bt, h, dk, dv), jnp.float32), jnp.zeros((bt, h, dk), jnp.float32),
            jnp.zeros((bt, h), jnp.float32))
    xs = (c_loc.astype(jnp.float32).transpose(1, 0, 2, 3, 4), n_loc.astype(jnp.float32).transpose(1, 0, 2, 3),
          m_loc.transpose(2, 0, 1), g.transpose(2, 0, 1))
    _, (c_in, n_in, m_in) = lax.scan(step, init, xs)
    d = b[..., :, None] - b[..., None, :] + log_i[..., None, :]
    d = jnp.where(jnp.tril(jnp.ones((L, L), dtype=bool)), d, -jnp.inf)
    m_inter = b + m_in.transpose(1, 2, 0)[..., None]
    m_t = jnp.maximum(m_inter, jnp.max(d, axis=-1))
    scores = jnp.einsum('bclhk,bcshk->bhcls', q, k) * jnp.exp(d - m_t[..., None])
    w_inter = jnp.exp(m_inter - m_t)
    num = (jnp.einsum('bhcls,bcshv->bclhv', scores, v)
           + jnp.einsum('bclhk,cbhkv,bhcl->bclhv', q, c_in, w_inter))
    den = jnp.sum(scores, axis=-1) + jnp.einsum('bclhk,cbhk,bhcl->bhcl', q, n_in, w_inter)
    den = jnp.maximum(jnp.abs(den), jnp.exp(-m_t))
    out = num / den.transpose(0, 2, 3, 1)[..., None]
    return out.reshape(bt, s, h, dv)


def mlstm_branch(qk, v, o, if_pre, conv_w, conv_b, if_bias, norm_w):
    bt, s, _ = v.shape
    qk = jax.nn.silu(causal_depthwise_conv(qk, conv_w, conv_b))
    q, k = jnp.split(qk, 2, axis=-1)
    gates = if_pre + if_bias
    i_pre, f_pre = gates[..., :MLSTM_N_HEADS], gates[..., MLSTM_N_HEADS:]
    h = mlstm_chunked(q.reshape(bt, s, MLSTM_N_HEADS, MLSTM_HEAD_QK),
                      k.reshape(bt, s, MLSTM_N_HEADS, MLSTM_HEAD_QK),
                      v.reshape(bt, s, MLSTM_N_HEADS, MLSTM_HEAD_V), i_pre, f_pre)
    h = rms_norm(h) * norm_w.reshape(MLSTM_N_HEADS, MLSTM_HEAD_V)
    h = jax.nn.sigmoid(o).reshape(bt, s, MLSTM_N_HEADS, MLSTM_HEAD_V) * h
    return h.reshape(bt, s, MLSTM_D_V)


def setup_inputs(seed: int = 0) -> dict:
    key = jax.random.key(seed)
    ks = jax.random.split(key, 24)
    nrm = lambda k, shape, scale: jax.random.normal(k, shape, jnp.float32) * scale
    dt_min, dt_max = 1e-3, 1e-1
    dt0 = jnp.exp(jax.random.uniform(ks[7], (DEPTH, SSD_N_HEADS), jnp.float32)
                  * (np.log(dt_max) - np.log(dt_min)) + np.log(dt_min))
    dt_bias = dt0 + jnp.log(-jnp.expm1(-dt0))
    a_log = jnp.log(jax.random.uniform(ks[8], (DEPTH, SSD_N_HEADS), jnp.float32, 1.0, 16.0))
    i_bias = nrm(ks[12], (DEPTH, MLSTM_N_HEADS), 0.1)
    f_bias = jnp.linspace(3.0, 6.0, MLSTM_N_HEADS, dtype=jnp.float32)[None, :] + nrm(ks[13], (DEPTH, MLSTM_N_HEADS), 0.1)
    return {
        "x": nrm(ks[0], (BATCH, SEQ, D_MODEL), 1.0),
        "c": nrm(ks[1], (BATCH, D_MODEL), 1.0),
        "w_ada": nrm(ks[2], (DEPTH, D_MODEL, 6 * D_MODEL), 0.5 * D_MODEL ** -0.5),
        "b_ada": nrm(ks[3], (DEPTH, 6 * D_MODEL), 0.02),
        "w_in": nrm(ks[4], (DEPTH, D_MODEL, D_IN_PROJ), D_MODEL ** -0.5),
        "ssd_conv_w": nrm(ks[5], (DEPTH, SSD_CONV, SSD_CONV_DIM), SSD_CONV ** -0.5),
        "ssd_conv_b": nrm(ks[6], (DEPTH, SSD_CONV_DIM), 0.02),
        "ssd_dt_bias": dt_bias,
        "ssd_a_log": a_log,
        "ssd_d": 1.0 + nrm(ks[9], (DEPTH, SSD_N_HEADS), 0.02),
        "ssd_norm_w": 1.0 + nrm(ks[10], (DEPTH, SSD_D_INNER), 0.02),
        "mlstm_conv_w": nrm(ks[11], (DEPTH, MLSTM_CONV, 2 * MLSTM_D_QK), MLSTM_CONV ** -0.5),
        "mlstm_conv_b": nrm(ks[14], (DEPTH, 2 * MLSTM_D_QK), 0.02),
        "mlstm_if_bias": jnp.concatenate([i_bias, f_bias], axis=-1),
        "mlstm_norm_w": 1.0 + nrm(ks[15], (DEPTH, MLSTM_D_V), 0.02),
        "w_branch_ssd": nrm(ks[16], (DEPTH, SSD_D_INNER, D_MODEL), SSD_D_INNER ** -0.5),
        "w_branch_mlstm": nrm(ks[17], (DEPTH, MLSTM_D_V, D_MODEL), MLSTM_D_V ** -0.5),
        "w_out": nrm(ks[18], (DEPTH, D_MODEL, D_MODEL), D_MODEL ** -0.5),
        "w_ffn_in": nrm(ks[19], (DEPTH, D_MODEL, 2 * D_FF), D_MODEL ** -0.5),
        "w_ffn_out": nrm(ks[20], (DEPTH, D_FF, D_MODEL), D_FF ** -0.5),
        "final_norm_w": 1.0 + nrm(ks[21], (D_MODEL,), 0.02),
    }


def reference(x, c, w_ada, b_ada, w_in, ssd_conv_w, ssd_conv_b, ssd_dt_bias, ssd_a_log, ssd_d,
              ssd_norm_w, mlstm_conv_w, mlstm_conv_b, mlstm_if_bias, mlstm_norm_w,
              w_branch_ssd, w_branch_mlstm, w_out, w_ffn_in, w_ffn_out, final_norm_w):
    c_act = jax.nn.silu(c)
    for l in range(DEPTH):
        mod = c_act @ w_ada[l] + b_ada[l]
        shift1, scale1, gate1, shift2, scale2, gate2 = [m[:, None, :] for m in jnp.split(mod, 6, axis=-1)]
        h = rms_norm(x) * (1.0 + scale1) + shift1
        proj = h @ w_in[l]
        z, xbc, dt, qk, v, o, if_pre, gate_pre = split_sizes(proj, IN_SPLITS)
        y_ssd = ssd_branch(z, xbc, dt, ssd_conv_w[l], ssd_conv_b[l], ssd_dt_bias[l],
                           ssd_a_log[l], ssd_d[l], ssd_norm_w[l])
        y_mlstm = mlstm_branch(qk, v, o, if_pre, mlstm_conv_w[l], mlstm_conv_b[l],
                               mlstm_if_bias[l], mlstm_norm_w[l])
        g_ssd, g_mlstm = jnp.split(jax.nn.sigmoid(gate_pre), 2, axis=-1)
        merged = g_ssd * (y_ssd @ w_branch_ssd[l]) + g_mlstm * (y_mlstm @ w_branch_mlstm[l])
        x = x + gate1 * (merged @ w_out[l])
        h2 = rms_norm(x) * (1.0 + scale2) + shift2
        gt, up = jnp.split(h2 @ w_ffn_in[l], 2, axis=-1)
        x = x + gate2 * ((jax.nn.silu(gt) * up) @ w_ffn_out[l])
    return rms_norm(x) * final_norm_w
```

```python
import functools

import numpy as np
import jax
import jax.numpy as jnp
from jax import lax
from jax.experimental import pallas as pl
from jax.experimental.pallas import tpu as pltpu

F32 = jnp.float32
BF16 = jnp.bfloat16
HIGHEST = lax.Precision.HIGHEST

LANES = 128
SUBLANES = 8
VMEM_LIMIT_BYTES = 56 * 2**20

RMS_EPS = 1e-6
CHUNK = 128
CONV_K = 4
HIST = SUBLANES
SSD_GROUPS = 4
SSD_STATE = 128
SSD_HEADDIM = 64
ML_HEADS = 4

NT_DIMS = (((1,), (1,)), ((), ()))
TN_DIMS = (((0,), (0,)), ((), ()))


def _dot(a, b):
    return jnp.dot(a, b, preferred_element_type=F32)


def _sigmoid(x):
    return jax.nn.sigmoid(x)


def _silu(x):
    return x * _sigmoid(x)


def _softplus(x):
    return jnp.maximum(x, 0.0) + jnp.log1p(jnp.exp(-jnp.abs(x)))


def _rms_scale(x):
    return lax.rsqrt(jnp.mean(x * x, axis=-1, keepdims=True) + RMS_EPS)


def _split_bf16(v):
    hi = v.astype(BF16)
    lo = (v - hi.astype(F32)).astype(BF16)
    return hi, lo


def _tril_mask(n):
    row = lax.broadcasted_iota(jnp.int32, (n, n), 0)
    col = lax.broadcasted_iota(jnp.int32, (n, n), 1)
    return row >= col


def _adaln_kernel(c_ref, w_ref, b_ref, o_ref):
    c = c_ref[...]
    o_ref[...] = jnp.dot(_silu(c), w_ref[...], precision=HIGHEST,
                         preferred_element_type=F32) + b_ref[...]


def _adaln(c, w_ada, b_ada):
    bsz, d = c.shape
    n = w_ada.shape[1]
    tn = d
    return pl.pallas_call(
        _adaln_kernel,
        grid=(n // tn,),
        in_specs=[pl.BlockSpec((bsz, d), lambda j: (0, 0)),
                  pl.BlockSpec((d, tn), lambda j: (0, j)),
                  pl.BlockSpec((1, tn), lambda j: (0, j))],
        out_specs=pl.BlockSpec((bsz, tn), lambda j: (0, j)),
        out_shape=jax.ShapeDtypeStruct((bsz, n), F32),
        compiler_params=pltpu.CompilerParams(dimension_semantics=("parallel",)),
        name="adaln",
    )(c, w_ada, b_ada.reshape(1, n))


def _in_proj_kernel(x_ref, mod_ref, w_ref, ws_ref, proj_ref, small_ref, h_ref, *, row_chunk):
    j = pl.program_id(1)
    tm = x_ref.shape[0]

    @pl.when(j == 0)
    def _():
        shift = mod_ref[0, 0:1, :]
        scale = 1.0 + mod_ref[0, 1:2, :]

        def body(r, carry):
            rows = pl.ds(pl.multiple_of(r * row_chunk, row_chunk), row_chunk)
            x = x_ref[rows, :]
            h = (x * _rms_scale(x)) * scale + shift
            hb = h.astype(BF16)
            h_ref[rows, :] = hb
            small_ref[rows, :] = _dot(hb, ws_ref[...])
            return carry

        lax.fori_loop(0, tm // row_chunk, body, 0)

    proj_ref[...] = _dot(h_ref[...], w_ref[...]).astype(BF16)


def _in_proj(x2d, mod3, w_main, w_small, seq):
    t, d = x2d.shape
    n = w_main.shape[1]
    tm, tn = seq, 1024
    return pl.pallas_call(
        functools.partial(_in_proj_kernel, row_chunk=256),
        grid=(t // tm, n // tn),
        in_specs=[pl.BlockSpec((tm, d), lambda i, j: (i, 0)),
                  pl.BlockSpec((1, 6, d), lambda i, j: (i, 0, 0)),
                  pl.BlockSpec((d, tn), lambda i, j: (0, j)),
                  pl.BlockSpec((d, LANES), lambda i, j: (0, 0))],
        out_specs=[pl.BlockSpec((tm, tn), lambda i, j: (i, j)),
                   pl.BlockSpec((tm, LANES), lambda i, j: (i, 0))],
        out_shape=[jax.ShapeDtypeStruct((t, n), BF16),
                   jax.ShapeDtypeStruct((t, LANES), F32)],
        scratch_shapes=[pltpu.VMEM((tm, d), BF16)],
        compiler_params=pltpu.CompilerParams(
            dimension_semantics=("parallel", "arbitrary"),
            vmem_limit_bytes=VMEM_LIMIT_BYTES),
        name="in_proj",
    )(x2d, mod3, w_main, w_small)


def _conv_silu(buf_ref, cw_ref, cb_ref, col0, ncols, rows):
    cols = slice(col0, col0 + ncols)
    acc = cb_ref[:, cols]
    for j in range(CONV_K):
        r0 = HIST - (CONV_K - 1) + j
        acc = acc + cw_ref[j:j + 1, cols] * buf_ref[r0:r0 + rows, cols]
    return _silu(acc)


def _ssd_kernel(z_ref, xs_ref, bm_ref, cm_ref, sm_ref, cw_ref, cb_ref, dtb_ref, alog_ref,
                dexp_ref, nw_ref, e_ref, out_ref, buf_ref, st_ref, xs_s, y_s):
    L = CHUNK
    G, N, P = SSD_GROUPS, SSD_STATE, SSD_HEADDIM
    d_inner = xs_ref.shape[1]
    gw = d_inner // G
    gn = G * N
    c = pl.program_id(1)

    @pl.when(c == 0)
    def _():
        buf_ref[0:HIST, :] = jnp.zeros((HIST, buf_ref.shape[1]), F32)
        st_ref[...] = jnp.zeros(st_ref.shape, F32)

    buf_ref[HIST:HIST + L, 0:d_inner] = xs_ref[...].astype(F32)
    buf_ref[HIST:HIST + L, d_inner:d_inner + gn] = bm_ref[...].astype(F32)
    buf_ref[HIST:HIST + L, d_inner + gn:d_inner + 2 * gn] = cm_ref[...].astype(F32)
    for t in range(d_inner // gw):
        xs_s[:, t * gw:(t + 1) * gw] = _conv_silu(buf_ref, cw_ref, cb_ref, t * gw, gw, L)
    bm = _conv_silu(buf_ref, cw_ref, cb_ref, d_inner, gn, L).astype(BF16)
    cm = _conv_silu(buf_ref, cw_ref, cb_ref, d_inner + gn, gn, L).astype(BF16)
    buf_ref[0:HIST, :] = buf_ref[L:L + HIST, :]

    lane = lax.broadcasted_iota(jnp.int32, (L, LANES), 1)
    n_heads = d_inner // P
    dt = jnp.where(lane < n_heads, _softplus(sm_ref[...] + dtb_ref[...]), 0.0)
    a = dt * (-jnp.exp(alog_ref[...]))
    tril = _tril_mask(L)
    acs = jnp.dot(tril.astype(F32), a, precision=HIGHEST, preferred_element_type=F32)
    acs_t = acs.T
    a_tot = acs[L - 1:L, :]
    w_dt = jnp.exp(a_tot - acs) * dt
    e_acs = jnp.exp(acs)
    e_tot = jnp.broadcast_to(jnp.exp(a_tot), (SUBLANES, LANES))
    stacked = jnp.concatenate([dt, e_acs, w_dt, e_tot], axis=0)
    s_hi, s_lo = _split_bf16(stacked)

    lane_p = lax.broadcasted_iota(jnp.int32, (L, LANES), 1)
    ssq = jnp.zeros((L, LANES), F32)
    for g in range(G):
        gcols = slice(g * gw, (g + 1) * gw)
        e_g = e_ref[:, gcols]
        ex = _dot(s_hi, e_g) + _dot(s_lo, e_g)
        dt_e, eacs_e, wdt_e = ex[0:L], ex[L:2 * L], ex[2 * L:3 * L]
        etot_e = ex[3 * L:3 * L + 1]
        xs_g = xs_s[:, gcols]
        xdt_g = xs_g * dt_e
        xw_g = (xs_g * wdt_e).astype(BF16)
        bm_g = bm[:, g * N:(g + 1) * N]
        cm_g = cm[:, g * N:(g + 1) * N]
        cb_g = lax.dot_general(cm_g, bm_g, NT_DIMS, preferred_element_type=F32)
        st_g = st_ref[g]
        y_off = _dot(cm_g, st_g.astype(BF16)) * eacs_e
        st_ref[g] = st_g * etot_e + lax.dot_general(bm_g, xw_g, TN_DIMS,
                                                    preferred_element_type=F32)
        heads_per_tile = LANES // P
        for p in range(gw // LANES):
            pc = slice(p * LANES, (p + 1) * LANES)
            ms = []
            for q in range(heads_per_tile):
                h = (g * gw + p * LANES) // P + q
                seg = acs[:, h:h + 1] - acs_t[h:h + 1, :]
                dec = jnp.exp(jnp.where(tril, seg, -jnp.inf))
                ms.append((cb_g * dec).astype(BF16))
            lhs = jnp.concatenate(ms, axis=1)
            xp = xdt_g[:, pc]
            rhs = jnp.concatenate(
                [jnp.where((lane_p >= q * P) & (lane_p < (q + 1) * P), xp, 0.0).astype(BF16)
                 for q in range(heads_per_tile)], axis=0)
            col = slice(g * gw + p * LANES, g * gw + (p + 1) * LANES)
            y = _dot(lhs, rhs) + y_off[:, pc] + xs_g[:, pc] * dexp_ref[:, col]
            gated = y * _silu(z_ref[:, col].astype(F32))
            y_s[:, col] = gated
            ssq = ssq + gated * gated

    inv = lax.rsqrt(jnp.sum(ssq, axis=-1, keepdims=True) / d_inner + RMS_EPS)
    out_ref[...] = (y_s[...] * inv * nw_ref[...]).astype(BF16)


def _ssd(proj, small, conv_w, conv_b, dtb_row, alog_row, dexp_row, nw_row, expand, bsz, seq,
         d_inner, col_idx):
    L = CHUNK
    nc = seq // L
    gn = SSD_GROUPS * SSD_STATE
    conv_dim = d_inner + 2 * gn
    row = lambda b, c: b * nc + c
    full = lambda shape: pl.BlockSpec(shape, lambda b, c: (0,) * len(shape))
    return pl.pallas_call(
        _ssd_kernel,
        grid=(bsz, nc),
        in_specs=[pl.BlockSpec((L, d_inner), lambda b, c: (row(b, c), col_idx["z"])),
                  pl.BlockSpec((L, d_inner), lambda b, c: (row(b, c), col_idx["xs"])),
                  pl.BlockSpec((L, gn), lambda b, c: (row(b, c), col_idx["bm"])),
                  pl.BlockSpec((L, gn), lambda b, c: (row(b, c), col_idx["cm"])),
                  pl.BlockSpec((L, LANES), lambda b, c: (row(b, c), 0)),
                  full((CONV_K, conv_dim)), full((1, conv_dim)),
                  full((1, LANES)), full((1, LANES)),
                  full((1, d_inner)), full((1, d_inner)),
                  full((LANES, d_inner))],
        out_specs=pl.BlockSpec((L, d_inner), lambda b, c: (row(b, c), 0)),
        out_shape=jax.ShapeDtypeStruct((bsz * seq, d_inner), BF16),
        scratch_shapes=[pltpu.VMEM((L + HIST, conv_dim), F32),
                        pltpu.VMEM((SSD_GROUPS, SSD_STATE, d_inner // SSD_GROUPS), F32),
                        pltpu.VMEM((L, d_inner), F32),
                        pltpu.VMEM((L, d_inner), F32)],
        compiler_params=pltpu.CompilerParams(
            dimension_semantics=("parallel", "arbitrary"),
            vmem_limit_bytes=VMEM_LIMIT_BYTES),
        name="ssd",
    )(proj, proj, proj, proj, small, conv_w, conv_b, dtb_row, alog_row, dexp_row, nw_row, expand)


def _mlstm_kernel(qk_ref, v_ref, o_ref, sm_ref, cw_ref, cb_ref, ifb_ref, nw_ref, out_ref,
                  buf_ref, c_ref, n_ref, m_ref, q_s, k_s, *, i_lane, f_lane):
    L = CHUNK
    H = ML_HEADS
    d_qk = qk_ref.shape[1] // 2
    d_v = v_ref.shape[1]
    dk, dv = d_qk // H, d_v // H
    c = pl.program_id(1)

    @pl.when(c == 0)
    def _():
        buf_ref[0:HIST, :] = jnp.zeros((HIST, buf_ref.shape[1]), F32)
        c_ref[...] = jnp.zeros(c_ref.shape, F32)
        n_ref[...] = jnp.zeros(n_ref.shape, F32)
        m_ref[...] = jnp.zeros(m_ref.shape, F32)

    buf_ref[HIST:HIST + L, :] = qk_ref[...].astype(F32)
    q_scale = dk ** -0.5
    for t in range(H):
        q_s[:, t * dk:(t + 1) * dk] = _conv_silu(buf_ref, cw_ref, cb_ref, t * dk, dk, L) * q_scale
        k_s[:, t * dk:(t + 1) * dk] = _conv_silu(buf_ref, cw_ref, cb_ref, d_qk + t * dk, dk, L)
    buf_ref[0:HIST, :] = buf_ref[L:L + HIST, :]

    gates = sm_ref[...] + ifb_ref[...]
    log_f = -_softplus(-gates)
    tril = _tril_mask(L)
    b_cum = jnp.dot(tril.astype(F32), log_f, precision=HIGHEST, preferred_element_type=F32)
    gates_t = gates.T
    b_cum_t = b_cum.T

    for h in range(H):
        li_col = gates[:, i_lane + h:i_lane + h + 1]
        li_row = gates_t[i_lane + h:i_lane + h + 1, :]
        b_col = b_cum[:, f_lane + h:f_lane + h + 1]
        b_row = b_cum_t[f_lane + h:f_lane + h + 1, :]
        g_tot = b_col[L - 1:L, :]
        m_prev = m_ref[h, 0:1, 0:1]
        n_prev = n_ref[h, 0:1, :]
        c_prev = c_ref[h]

        q = q_s[:, h * dk:(h + 1) * dk]
        k = k_s[:, h * dk:(h + 1) * dk]
        qb = q.astype(BF16)
        kb = k.astype(BF16)
        vb = v_ref[:, h * dv:(h + 1) * dv]

        d = jnp.where(tril, (b_col - b_row) + li_row, -jnp.inf)
        m_inter = b_col + m_prev
        m_t = jnp.maximum(m_inter, jnp.max(d, axis=-1, keepdims=True))
        scores = lax.dot_general(qb, kb, NT_DIMS, preferred_element_type=F32) * jnp.exp(d - m_t)
        w_inter = jnp.exp(m_inter - m_t)
        num = _dot(scores.astype(BF16), vb) + w_inter * _dot(qb, c_prev.astype(BF16))
        den = (jnp.sum(scores, axis=-1, keepdims=True)
               + w_inter * jnp.sum(q * n_prev, axis=-1, keepdims=True))
        den = jnp.maximum(jnp.abs(den), jnp.exp(-m_t))
        hid = num / den

        a_loc = (g_tot - b_col) + li_col
        m_loc = jnp.max(a_loc, axis=0, keepdims=True)
        kw = k * jnp.exp(a_loc - m_loc)
        c_loc = lax.dot_general(kw.astype(BF16), vb, TN_DIMS, preferred_element_type=F32)
        n_loc = jnp.sum(kw, axis=0, keepdims=True)
        m_new = jnp.maximum(g_tot + m_prev, m_loc)
        sp = jnp.exp(g_tot + m_prev - m_new)
        sc = jnp.exp(m_loc - m_new)
        c_ref[h] = sp * c_prev + sc * c_loc
        n_ref[h, 0:1, :] = sp * n_prev + sc * n_loc
        m_ref[h] = jnp.broadcast_to(m_new, m_ref.shape[1:])

        vcols = slice(h * dv, (h + 1) * dv)
        hn = hid * _rms_scale(hid) * nw_ref[:, vcols]
        out_ref[:, vcols] = (_sigmoid(o_ref[:, vcols].astype(F32)) * hn).astype(BF16)


def _mlstm(proj, small, conv_w, conv_b, ifb_row, nw_row, bsz, seq, d_qk, d_v, col_idx,
           i_lane, f_lane):
    L = CHUNK
    nc = seq // L
    H = ML_HEADS
    row = lambda b, c: b * nc + c
    full = lambda shape: pl.BlockSpec(shape, lambda b, c: (0,) * len(shape))
    return pl.pallas_call(
        functools.partial(_mlstm_kernel, i_lane=i_lane, f_lane=f_lane),
        grid=(bsz, nc),
        in_specs=[pl.BlockSpec((L, 2 * d_qk), lambda b, c: (row(b, c), col_idx["qk"])),
                  pl.BlockSpec((L, d_v), lambda b, c: (row(b, c), col_idx["v"])),
                  pl.BlockSpec((L, d_v), lambda b, c: (row(b, c), col_idx["o"])),
                  pl.BlockSpec((L, LANES), lambda b, c: (row(b, c), 0)),
                  full((CONV_K, 2 * d_qk)), full((1, 2 * d_qk)),
                  full((1, LANES)), full((1, d_v))],
        out_specs=pl.BlockSpec((L, d_v), lambda b, c: (row(b, c), 0)),
        out_shape=jax.ShapeDtypeStruct((bsz * seq, d_v), BF16),
        scratch_shapes=[pltpu.VMEM((L + HIST, 2 * d_qk), F32),
                        pltpu.VMEM((H, d_qk // H, d_v // H), F32),
                        pltpu.VMEM((H, SUBLANES, d_qk // H), F32),
                        pltpu.VMEM((H, SUBLANES, LANES), F32),
                        pltpu.VMEM((L, d_qk), F32),
                        pltpu.VMEM((L, d_qk), F32)],
        compiler_params=pltpu.CompilerParams(
            dimension_semantics=("parallel", "arbitrary"),
            vmem_limit_bytes=VMEM_LIMIT_BYTES),
        name="mlstm",
    )(proj, proj, proj, small, conv_w, conv_b, ifb_row, nw_row)


def _merge_kernel(ys_ref, ym_ref, gs_ref, gm_ref, x_ref, mod_ref, wbs_ref, wbm_ref, wo_ref,
                  out_ref):
    a = _dot(ys_ref[...], wbs_ref[...])
    b = _dot(ym_ref[...], wbm_ref[...])
    merged = (_sigmoid(gs_ref[...].astype(F32)) * a + _sigmoid(gm_ref[...].astype(F32)) * b)
    o = _dot(merged.astype(BF16), wo_ref[...])
    out_ref[...] = x_ref[...] + mod_ref[0, 2:3, :] * o


def _merge(y_ssd, y_ml, proj, x2d, mod3, wbs, wbm, wo, seq, col_idx):
    t, d = x2d.shape
    dv = y_ssd.shape[1]
    tm = 512
    per_b = seq // tm
    const = lambda shape: pl.BlockSpec(shape, lambda i: (0,) * len(shape),
                                       pipeline_mode=pl.Buffered(1))
    return pl.pallas_call(
        _merge_kernel,
        grid=(t // tm,),
        in_specs=[pl.BlockSpec((tm, dv), lambda i: (i, 0)),
                  pl.BlockSpec((tm, dv), lambda i: (i, 0)),
                  pl.BlockSpec((tm, d), lambda i: (i, col_idx["g_ssd"])),
                  pl.BlockSpec((tm, d), lambda i: (i, col_idx["g_ml"])),
                  pl.BlockSpec((tm, d), lambda i: (i, 0)),
                  pl.BlockSpec((1, 6, d), lambda i: (i // per_b, 0, 0)),
                  const((dv, d)), const((dv, d)), const((d, d))],
        out_specs=pl.BlockSpec((tm, d), lambda i: (i, 0)),
        out_shape=jax.ShapeDtypeStruct((t, d), F32),
        compiler_params=pltpu.CompilerParams(
            dimension_semantics=("parallel",),
            vmem_limit_bytes=VMEM_LIMIT_BYTES),
        name="merge",
    )(y_ssd, y_ml, proj, proj, x2d, mod3, wbs, wbm, wo)


def _ffn_kernel(x_ref, mod_ref, wi_ref, wo_ref, fw_ref, out_ref, h_s, acc_s, *, ff_chunk):
    x1 = x_ref[...]
    shift = mod_ref[0, 3:4, :]
    scale = 1.0 + mod_ref[0, 4:5, :]
    gate = mod_ref[0, 5:6, :]
    h_s[...] = ((x1 * _rms_scale(x1)) * scale + shift).astype(BF16)
    n_chunks = wo_ref.shape[0] // ff_chunk
    for c in range(n_chunks):
        gu = _dot(h_s[...], wi_ref[:, 2 * c * ff_chunk:2 * (c + 1) * ff_chunk])
        act = (_silu(gu[:, :ff_chunk]) * gu[:, ff_chunk:]).astype(BF16)
        part = _dot(act, wo_ref[c * ff_chunk:(c + 1) * ff_chunk, :])
        if c == 0:
            acc_s[...] = part
        else:
            acc_s[...] += part
    x2 = x1 + gate * acc_s[...]
    out_ref[...] = (x2 * _rms_scale(x2)) * fw_ref[...]


def _ffn(x1, mod3, wi, wo, fw_row, seq, ff_chunk):
    t, d = x1.shape
    tm = 1024
    per_b = seq // tm
    const = lambda shape: pl.BlockSpec(shape, lambda i: (0,) * len(shape),
                                       pipeline_mode=pl.Buffered(1))
    return pl.pallas_call(
        functools.partial(_ffn_kernel, ff_chunk=ff_chunk),
        grid=(t // tm,),
        in_specs=[pl.BlockSpec((tm, d), lambda i: (i, 0)),
                  pl.BlockSpec((1, 6, d), lambda i: (i // per_b, 0, 0)),
                  const(wi.shape), const(wo.shape), const((1, d))],
        out_specs=pl.BlockSpec((tm, d), lambda i: (i, 0)),
        out_shape=jax.ShapeDtypeStruct((t, d), F32),
        scratch_shapes=[pltpu.VMEM((tm, d), BF16), pltpu.VMEM((tm, d), F32)],
        compiler_params=pltpu.CompilerParams(
            dimension_semantics=("parallel",),
            vmem_limit_bytes=VMEM_LIMIT_BYTES),
        name="ffn",
    )(x1, mod3, wi, wo, fw_row)


def _pad_lanes(row, offset=0):
    return jnp.zeros((1, LANES), F32).at[0, offset:offset + row.shape[0]].set(row)


def kernel(x, c, w_ada, b_ada, w_in, ssd_conv_w, ssd_conv_b, ssd_dt_bias, ssd_a_log, ssd_d,
           ssd_norm_w, mlstm_conv_w, mlstm_conv_b, mlstm_if_bias, mlstm_norm_w,
           w_branch_ssd, w_branch_mlstm, w_out, w_ffn_in, w_ffn_out, final_norm_w):
    bsz, seq, d = x.shape
    depth = w_ada.shape[0]
    d_inner = ssd_norm_w.shape[-1]
    n_heads = ssd_dt_bias.shape[-1]
    gn = SSD_GROUPS * SSD_STATE
    d_qk = mlstm_conv_w.shape[-1] // 2
    d_v = mlstm_norm_w.shape[-1]
    d_ff = w_ffn_out.shape[1]
    assert ssd_conv_w.shape[-1] == d_inner + 2 * gn and d_inner == n_heads * SSD_HEADDIM
    assert n_heads + 2 * ML_HEADS <= LANES and seq % CHUNK == 0
    assert d_inner == 2 * d and d_v == 2 * d and 2 * d_qk == 2 * d and gn * 2 == d

    sizes = (d_inner, d_inner + 2 * gn, n_heads, 2 * d_qk, d_v, d_v, 2 * ML_HEADS, 2 * d)
    offs = np.concatenate([[0], np.cumsum(sizes)])
    assert w_in.shape[-1] == offs[-1]
    col_idx = {"z": 0, "xs": 1, "qk": 2, "v": 3, "o": 4,
               "g_ssd": 10, "g_ml": 11, "bm": 24, "cm": 25}
    i_lane, f_lane = n_heads, n_heads + ML_HEADS
    ff_chunk = 256

    expand = (jnp.arange(LANES)[:, None] == (jnp.arange(d_inner)[None, :] // SSD_HEADDIM)
              ).astype(BF16)

    x2d = x.reshape(bsz * seq, d)
    for l in range(depth):
        wl = w_in[l]
        seg = lambda k: wl[:, offs[k]:offs[k + 1]]
        xbc_w = seg(1)
        w_main = jnp.concatenate(
            [seg(0), xbc_w[:, :d_inner], seg(3), seg(4), seg(5), seg(7), xbc_w[:, d_inner:]],
            axis=1).astype(BF16)
        w_small = jnp.concatenate(
            [seg(2), seg(6), jnp.zeros((d, LANES - n_heads - 2 * ML_HEADS), F32)],
            axis=1).astype(BF16)
        wi = jnp.stack([w_ffn_in[l][:, :d_ff].reshape(d, d_ff // ff_chunk, ff_chunk),
                        w_ffn_in[l][:, d_ff:].reshape(d, d_ff // ff_chunk, ff_chunk)],
                       axis=2).reshape(d, 2 * d_ff).astype(BF16)

        mod3 = _adaln(c, w_ada[l], b_ada[l]).reshape(bsz, 6, d)
        proj, small = _in_proj(x2d, mod3, w_main, w_small, seq)
        y_ssd = _ssd(proj, small, ssd_conv_w[l], ssd_conv_b[l].reshape(1, -1),
                     _pad_lanes(ssd_dt_bias[l]), _pad_lanes(ssd_a_log[l]),
                     jnp.repeat(ssd_d[l], SSD_HEADDIM).reshape(1, d_inner),
                     ssd_norm_w[l].reshape(1, d_inner), expand, bsz, seq, d_inner, col_idx)
        y_ml = _mlstm(proj, small, mlstm_conv_w[l], mlstm_conv_b[l].reshape(1, -1),
                      _pad_lanes(mlstm_if_bias[l], i_lane), mlstm_norm_w[l].reshape(1, d_v),
                      bsz, seq, d_qk, d_v, col_idx, i_lane, f_lane)
        x1 = _merge(y_ssd, y_ml, proj, x2d, mod3, w_branch_ssd[l].astype(BF16),
                    w_branch_mlstm[l].astype(BF16), w_out[l].astype(BF16), seq, col_idx)
        assert l == depth - 1 == 0
        x2d = _ffn(x1, mod3, wi, w_ffn_out[l].astype(BF16), final_norm_w.reshape(1, d), seq,
                   ff_chunk)
    return x2d.reshape(bsz, seq, d)
```

```python
import functools

import numpy as np
import jax
import jax.numpy as jnp
from jax import lax
from jax.experimental import pallas as pl
from jax.experimental.pallas import tpu as pltpu

F32 = jnp.float32
BF16 = jnp.bfloat16
HIGHEST = lax.Precision.HIGHEST

LANES = 128
SUBLANES = 8
VMEM_LIMIT_BYTES = 56 * 2**20

RMS_EPS = 1e-6
LOG2E = 1.4426950408889634
CHUNK = 128
CONV_K = 4
HIST = SUBLANES
IN_PROJ_TN = 1024
MIX_BATCH = 2
SSD_GROUPS = 4
SSD_STATE = 128
SSD_HEADDIM = 64
ML_HEADS = 4

NT_DIMS = (((1,), (1,)), ((), ()))
TN_DIMS = (((0,), (0,)), ((), ()))


def _dot(a, b):
    return jnp.dot(a, b, preferred_element_type=F32)


def _sigmoid(x):
    return jax.nn.sigmoid(x)


def _silu(x):
    return x * _sigmoid(x)


def _softplus(x):
    return jnp.maximum(x, 0.0) + jnp.log1p(jnp.exp(-jnp.abs(x)))


def _rms_scale(x):
    return lax.rsqrt(jnp.mean(x * x, axis=-1, keepdims=True) + RMS_EPS)


def _split_bf16(v):
    hi = v.astype(BF16)
    lo = (v - hi.astype(F32)).astype(BF16)
    return hi, lo


def _tril_mask(n):
    row = lax.broadcasted_iota(jnp.int32, (n, n), 0)
    col = lax.broadcasted_iota(jnp.int32, (n, n), 1)
    return row >= col


def _adaln_kernel(c_ref, w_ref, b_ref, o_ref):
    c = c_ref[...]
    o_ref[...] = jnp.dot(_silu(c), w_ref[...], precision=HIGHEST,
                         preferred_element_type=F32) + b_ref[...]


def _adaln(c, w_ada, b_ada):
    bsz, d = c.shape
    n = w_ada.shape[1]
    tn = d
    return pl.pallas_call(
        _adaln_kernel,
        grid=(n // tn,),
        in_specs=[pl.BlockSpec((bsz, d), lambda j: (0, 0)),
                  pl.BlockSpec((d, tn), lambda j: (0, j)),
                  pl.BlockSpec((1, tn), lambda j: (0, j))],
        out_specs=pl.BlockSpec((bsz, tn), lambda j: (0, j)),
        out_shape=jax.ShapeDtypeStruct((bsz, n), F32),
        compiler_params=pltpu.CompilerParams(dimension_semantics=("parallel",)),
        name="adaln",
    )(c, w_ada, b_ada.reshape(1, n))


def _conv_silu(buf_ref, cw_ref, cb_ref, t, r0, rows):
    cols = slice(t * LANES, (t + 1) * LANES)
    acc = cb_ref[:, cols]
    for j in range(CONV_K):
        s = HIST - (CONV_K - 1) + j + r0
        acc = acc + cw_ref[j:j + 1, cols] * buf_ref[t, s:s + rows, :]
    return _silu(acc)


def _in_proj_kernel(x_ref, mod_ref, w_ref, ws_ref, cw_ref, cb_ref, proj_ref, small_ref, h_ref,
                    buf_ref, *, norm_rows, row_block, conv_rows, conv_tiles):
    j = pl.program_id(1)
    tm = x_ref.shape[0]
    tn = w_ref.shape[1]

    @pl.when(j == 0)
    def _():
        shift = mod_ref[0, 0:1, :]
        scale = 1.0 + mod_ref[0, 1:2, :]

        def body(r, carry):
            rows = pl.ds(pl.multiple_of(r * norm_rows, norm_rows), norm_rows)
            x = x_ref[rows, :]
            h = (x * _rms_scale(x)) * scale + shift
            hb = h.astype(BF16)
            h_ref[rows, :] = hb
            small_ref[rows, :] = _dot(hb, ws_ref[...])
            return carry

        lax.fori_loop(0, tm // norm_rows, body, 0)

    is_conv = functools.reduce(jnp.logical_or, [j == t for t in conv_tiles])

    @pl.when(jnp.logical_not(is_conv))
    def _():
        for rb in range(tm // row_block):
            rows = slice(rb * row_block, (rb + 1) * row_block)
            proj_ref[rows, :] = _dot(h_ref[rows, :], w_ref[...]).astype(BF16)

    @pl.when(is_conv)
    def _():
        nblk = tn // LANES
        for t in range(nblk):
            buf_ref[t, 0:HIST, :] = jnp.zeros((HIST, LANES), F32)
        for rb in range(0, tm, row_block):
            res = _dot(h_ref[rb:rb + row_block, :], w_ref[...])
            for t in range(nblk):
                buf_ref[t, HIST:HIST + row_block, :] = res[:, t * LANES:(t + 1) * LANES]
            for t in range(nblk):
                for r0 in range(0, row_block, conv_rows):
                    proj_ref[rb + r0:rb + r0 + conv_rows, t * LANES:(t + 1) * LANES] = _conv_silu(
                        buf_ref, cw_ref, cb_ref, t, r0, conv_rows).astype(BF16)
            for t in range(nblk):
                buf_ref[t, 0:HIST, :] = buf_ref[t, row_block:row_block + HIST, :]


def _in_proj(x2d, mod3, w_main, w_small, conv_w, conv_b, seq, conv_tiles):
    t, d = x2d.shape
    n = w_main.shape[1]
    tm, tn = seq, IN_PROJ_TN
    row_block = 512
    return pl.pallas_call(
        functools.partial(_in_proj_kernel, norm_rows=256, row_block=row_block, conv_rows=256,
                          conv_tiles=conv_tiles),
        grid=(t // tm, n // tn),
        in_specs=[pl.BlockSpec((tm, d), lambda i, j: (i, 0)),
                  pl.BlockSpec((1, 6, d), lambda i, j: (i, 0, 0)),
                  pl.BlockSpec((d, tn), lambda i, j: (0, j)),
                  pl.BlockSpec((d, LANES), lambda i, j: (0, 0)),
                  pl.BlockSpec((CONV_K, tn), lambda i, j: (0, j)),
                  pl.BlockSpec((1, tn), lambda i, j: (0, j))],
        out_specs=[pl.BlockSpec((tm, tn), lambda i, j: (i, j)),
                   pl.BlockSpec((tm, LANES), lambda i, j: (i, 0))],
        out_shape=[jax.ShapeDtypeStruct((t, n), BF16),
                   jax.ShapeDtypeStruct((t, LANES), F32)],
        scratch_shapes=[pltpu.VMEM((tm, d), BF16),
                        pltpu.VMEM((tn // LANES, HIST + row_block, LANES), F32)],
        compiler_params=pltpu.CompilerParams(
            dimension_semantics=("parallel", "arbitrary"),
            vmem_limit_bytes=VMEM_LIMIT_BYTES),
        name="in_proj",
    )(x2d, mod3, w_main, w_small, conv_w, conv_b)


def _ssd_kernel(z_ref, xs_ref, bm_ref, cm_ref, sm_ref, dtb_ref, alog_ref, dexp_ref, nw_ref,
                e_ref, out_ref, st_ref, y_s):
    @pl.when(pl.program_id(1) == 0)
    def _():
        st_ref[...] = jnp.zeros(st_ref.shape, F32)

    for s in range(z_ref.shape[0]):
        _ssd_chunk(z_ref.at[s], xs_ref.at[s], bm_ref.at[s], cm_ref.at[s], sm_ref.at[s], dtb_ref,
                   alog_ref, dexp_ref, nw_ref, e_ref, out_ref.at[s], st_ref.at[s], y_s.at[s])


def _ssd_chunk(z_ref, xs_ref, bm_ref, cm_ref, sm_ref, dtb_ref, alog_ref, dexp_ref, nw_ref,
               e_ref, out_ref, st_ref, y_s):
    L = CHUNK
    G, N, P = SSD_GROUPS, SSD_STATE, SSD_HEADDIM
    d_inner = xs_ref.shape[1]
    gw = d_inner // G

    lane = lax.broadcasted_iota(jnp.int32, (L, LANES), 1)
    n_heads = d_inner // P
    dt = jnp.where(lane < n_heads, _softplus(sm_ref[...] + dtb_ref[...]), 0.0)
    a = dt * (-jnp.exp(alog_ref[...]))
    tril = _tril_mask(L)
    acs = jnp.dot(tril.astype(F32), a, precision=HIGHEST, preferred_element_type=F32)
    acs2 = acs * LOG2E
    acs2_t = acs2.T
    a_tot = acs[L - 1:L, :]
    w_dt = jnp.exp(a_tot - acs) * dt
    e_acs = jnp.exp2(acs2)
    e_tot = jnp.broadcast_to(jnp.exp(a_tot), (SUBLANES, LANES))
    stacked = jnp.concatenate([dt, e_acs, w_dt, e_tot], axis=0)
    s_cat = jnp.concatenate(_split_bf16(stacked), axis=1)

    heads_per_tile = LANES // P
    lane_row = lax.broadcasted_iota(jnp.int32, (1, LANES), 1)
    head_mask = [((lane_row >= q * P) & (lane_row < (q + 1) * P)).astype(BF16)
                 for q in range(heads_per_tile)]
    ssq = jnp.zeros((L, LANES), F32)
    for g in range(G):
        gcols = slice(g * gw, (g + 1) * gw)
        ex = _dot(s_cat, e_ref[:, gcols])
        dt_e, eacs_e, wdt_e = ex[0:L], ex[L:2 * L], ex[2 * L:3 * L]
        etot_e = ex[3 * L:3 * L + 1]
        xs_g = xs_ref[:, gcols].astype(F32)
        xdt_g = (xs_g * dt_e).astype(BF16)
        xw_g = (xs_g * wdt_e).astype(BF16)
        bm_g = bm_ref[:, g * N:(g + 1) * N]
        cm_g = cm_ref[:, g * N:(g + 1) * N]
        cb_g = lax.dot_general(cm_g, bm_g, NT_DIMS, preferred_element_type=F32)
        st_g = st_ref[g]
        y_off = _dot(cm_g, st_g.astype(BF16)) * eacs_e
        st_ref[g] = st_g * etot_e + lax.dot_general(bm_g, xw_g, TN_DIMS,
                                                    preferred_element_type=F32)
        for p in range(gw // LANES):
            pc = slice(p * LANES, (p + 1) * LANES)
            ms = []
            for q in range(heads_per_tile):
                h = (g * gw + p * LANES) // P + q
                seg = acs2[:, h:h + 1] - acs2_t[h:h + 1, :]
                dec = jnp.exp2(jnp.where(tril, seg, -jnp.inf))
                ms.append((cb_g * dec).astype(BF16))
            lhs = jnp.concatenate(ms, axis=1)
            xp = xdt_g[:, pc]
            rhs = jnp.concatenate([xp * m for m in head_mask], axis=0)
            col = slice(g * gw + p * LANES, g * gw + (p + 1) * LANES)
            y = _dot(lhs, rhs) + y_off[:, pc] + xs_g[:, pc] * dexp_ref[:, col]
            gated = y * _silu(z_ref[:, col].astype(F32))
            y_s[:, col] = gated
            ssq = ssq + gated * gated

    inv = lax.rsqrt(jnp.sum(ssq, axis=-1, keepdims=True) / d_inner + RMS_EPS)
    out_ref[...] = (y_s[...] * inv * nw_ref[...]).astype(BF16)


def _ssd(proj, small, dtb_row, alog_row, dexp_row, nw_row, expand, d_inner, col_idx):
    bsz, seq, _ = proj.shape
    L, nb = CHUNK, MIX_BATCH
    gn = SSD_GROUPS * SSD_STATE
    blk = lambda width, col: pl.BlockSpec((nb, L, width), lambda b, c: (b, c, col))
    full = lambda shape: pl.BlockSpec(shape, lambda b, c: (0,) * len(shape))
    return pl.pallas_call(
        _ssd_kernel,
        grid=(bsz // nb, seq // L),
        in_specs=[blk(d_inner, col_idx["z"]), blk(d_inner, col_idx["xs"]),
                  blk(gn, col_idx["bm"]), blk(gn, col_idx["cm"]), blk(LANES, 0),
                  full((1, LANES)), full((1, LANES)),
                  full((1, d_inner)), full((1, d_inner)),
                  full((2 * LANES, d_inner))],
        out_specs=blk(d_inner, 0),
        out_shape=jax.ShapeDtypeStruct((bsz, seq, d_inner), BF16),
        scratch_shapes=[pltpu.VMEM((nb, SSD_GROUPS, SSD_STATE, d_inner // SSD_GROUPS), F32),
                        pltpu.VMEM((nb, L, d_inner), F32)],
        compiler_params=pltpu.CompilerParams(
            dimension_semantics=("parallel", "arbitrary"),
            vmem_limit_bytes=VMEM_LIMIT_BYTES),
        name="ssd",
    )(proj, proj, proj, proj, small, dtb_row, alog_row, dexp_row, nw_row, expand)


def _mlstm_kernel(qk_ref, v_ref, o_ref, sm_ref, ifb_ref, nw_ref, out_ref, c_ref, n_ref, m_ref,
                  *, i_lane, f_lane):
    @pl.when(pl.program_id(1) == 0)
    def _():
        c_ref[...] = jnp.zeros(c_ref.shape, F32)
        n_ref[...] = jnp.zeros(n_ref.shape, F32)
        m_ref[...] = jnp.zeros(m_ref.shape, F32)

    for s in range(qk_ref.shape[0]):
        _mlstm_chunk(qk_ref.at[s], v_ref.at[s], o_ref.at[s], sm_ref.at[s], ifb_ref, nw_ref,
                     out_ref.at[s], c_ref.at[s], n_ref.at[s], m_ref.at[s], i_lane, f_lane)


def _mlstm_chunk(qk_ref, v_ref, o_ref, sm_ref, ifb_ref, nw_ref, out_ref, c_ref, n_ref, m_ref,
                 i_lane, f_lane):
    L = CHUNK
    H = ML_HEADS
    d_qk = qk_ref.shape[1] // 2
    d_v = v_ref.shape[1]
    dk, dv = d_qk // H, d_v // H

    q_scale = dk ** -0.5
    assert 2.0 ** round(np.log2(q_scale)) == q_scale

    gates = sm_ref[...] + ifb_ref[...]
    log_f = -_softplus(-gates)
    tril = _tril_mask(L)
    b_cum = jnp.dot(tril.astype(F32), log_f, precision=HIGHEST, preferred_element_type=F32)
    gates_t = gates.T
    b_cum_t = b_cum.T

    for h in range(H):
        li_col = gates[:, i_lane + h:i_lane + h + 1]
        li_row = gates_t[i_lane + h:i_lane + h + 1, :]
        b_col = b_cum[:, f_lane + h:f_lane + h + 1]
        b_row = b_cum_t[f_lane + h:f_lane + h + 1, :]
        g_tot = b_col[L - 1:L, :]
        m_prev = m_ref[h, 0:1, 0:1]
        n_prev = n_ref[h, 0:1, :]
        c_prev = c_ref[h]

        qb = qk_ref[:, h * dk:(h + 1) * dk] * q_scale
        kb = qk_ref[:, d_qk + h * dk:d_qk + (h + 1) * dk]
        q = qb.astype(F32)
        k = kb.astype(F32)
        vb = v_ref[:, h * dv:(h + 1) * dv]

        d = jnp.where(tril, (b_col - b_row) + li_row, -jnp.inf)
        m_inter = b_col + m_prev
        m_t = jnp.maximum(m_inter, jnp.max(d, axis=-1, keepdims=True))
        scores = lax.dot_general(qb, kb, NT_DIMS, preferred_element_type=F32) * jnp.exp(d - m_t)
        w_inter = jnp.exp(m_inter - m_t)
        lhs = jnp.concatenate([scores.astype(BF16), (q * w_inter).astype(BF16)], axis=1)
        rhs = jnp.concatenate([vb, c_prev.astype(BF16)], axis=0)
        num = _dot(lhs, rhs)
        den = (jnp.sum(scores, axis=-1, keepdims=True)
               + w_inter * jnp.sum(q * n_prev, axis=-1, keepdims=True))
        den = jnp.maximum(jnp.abs(den), jnp.exp(-m_t))

        a_loc = (g_tot - b_col) + li_col
        m_loc = jnp.max(a_loc, axis=0, keepdims=True)
        m_new = jnp.maximum(g_tot + m_prev, m_loc)
        sp = jnp.exp(g_tot + m_prev - m_new)
        kw = k * jnp.exp(a_loc - m_new)
        c_ref[h] = sp * c_prev + lax.dot_general(kw.astype(BF16), vb, TN_DIMS,
                                                 preferred_element_type=F32)
        n_ref[h, 0:1, :] = sp * n_prev + jnp.sum(kw, axis=0, keepdims=True)
        m_ref[h] = jnp.broadcast_to(m_new, m_ref.shape[1:])

        vcols = slice(h * dv, (h + 1) * dv)
        inv_den = 1.0 / den
        msq = jnp.mean(num * num, axis=-1, keepdims=True)
        row_scale = inv_den * lax.rsqrt(inv_den * inv_den * msq + RMS_EPS)
        gate = _sigmoid(o_ref[:, vcols].astype(F32))
        out_ref[:, vcols] = (gate * (num * row_scale * nw_ref[:, vcols])).astype(BF16)


def _mlstm(proj, small, ifb_row, nw_row, d_qk, d_v, col_idx, i_lane, f_lane):
    bsz, seq, _ = proj.shape
    L, nb = CHUNK, MIX_BATCH
    H = ML_HEADS
    blk = lambda width, col: pl.BlockSpec((nb, L, width), lambda b, c: (b, c, col))
    full = lambda shape: pl.BlockSpec(shape, lambda b, c: (0,) * len(shape))
    return pl.pallas_call(
        functools.partial(_mlstm_kernel, i_lane=i_lane, f_lane=f_lane),
        grid=(bsz // nb, seq // L),
        in_specs=[blk(2 * d_qk, col_idx["qk"]), blk(d_v, col_idx["v"]), blk(d_v, col_idx["o"]),
                  blk(LANES, 0), full((1, LANES)), full((1, d_v))],
        out_specs=blk(d_v, 0),
        out_shape=jax.ShapeDtypeStruct((bsz, seq, d_v), BF16),
        scratch_shapes=[pltpu.VMEM((nb, H, d_qk // H, d_v // H), F32),
                        pltpu.VMEM((nb, H, SUBLANES, d_qk // H), F32),
                        pltpu.VMEM((nb, H, SUBLANES, LANES), F32)],
        compiler_params=pltpu.CompilerParams(
            dimension_semantics=("parallel", "arbitrary"),
            vmem_limit_bytes=VMEM_LIMIT_BYTES),
        name="mlstm",
    )(proj, proj, proj, small, ifb_row, nw_row)


def _merge_kernel(ys_ref, ym_ref, gs_ref, gm_ref, x_ref, mod_ref, wbs_ref, wbm_ref, wo_ref,
                  out_ref):
    a = _dot(ys_ref[...], wbs_ref[...])
    b = _dot(ym_ref[...], wbm_ref[...])
    merged = (_sigmoid(gs_ref[...].astype(F32)) * a + _sigmoid(gm_ref[...].astype(F32)) * b)
    o = _dot(merged.astype(BF16), wo_ref[...])
    out_ref[...] = x_ref[...] + mod_ref[0, 2:3, :] * o


def _merge(y_ssd, y_ml, proj, x2d, mod3, wbs, wbm, wo, seq, col_idx):
    t, d = x2d.shape
    dv = y_ssd.shape[1]
    tm = 512
    per_b = seq // tm
    const = lambda shape: pl.BlockSpec(shape, lambda i: (0,) * len(shape),
                                       pipeline_mode=pl.Buffered(1))
    return pl.pallas_call(
        _merge_kernel,
        grid=(t // tm,),
        in_specs=[pl.BlockSpec((tm, dv), lambda i: (i, 0)),
                  pl.BlockSpec((tm, dv), lambda i: (i, 0)),
                  pl.BlockSpec((tm, d), lambda i: (i, col_idx["g_ssd"])),
                  pl.BlockSpec((tm, d), lambda i: (i, col_idx["g_ml"])),
                  pl.BlockSpec((tm, d), lambda i: (i, 0)),
                  pl.BlockSpec((1, 6, d), lambda i: (i // per_b, 0, 0)),
                  const((dv, d)), const((dv, d)), const((d, d))],
        out_specs=pl.BlockSpec((tm, d), lambda i: (i, 0)),
        out_shape=jax.ShapeDtypeStruct((t, d), F32),
        compiler_params=pltpu.CompilerParams(
            dimension_semantics=("parallel",),
            vmem_limit_bytes=VMEM_LIMIT_BYTES),
        name="merge",
    )(y_ssd, y_ml, proj, proj, x2d, mod3, wbs, wbm, wo)


def _ffn_kernel(x_ref, mod_ref, wi_ref, wo_ref, fw_ref, out_ref, h_s, acc_s, *, ff_chunk):
    x1 = x_ref[...]
    shift = mod_ref[0, 3:4, :]
    scale = 1.0 + mod_ref[0, 4:5, :]
    gate = mod_ref[0, 5:6, :]
    h_s[...] = ((x1 * _rms_scale(x1)) * scale + shift).astype(BF16)
    d_ff = wo_ref.shape[0]
    for c in range(d_ff // ff_chunk):
        cc = slice(c * ff_chunk, (c + 1) * ff_chunk)
        gt = _dot(h_s[...], wi_ref[:, cc])
        up = _dot(h_s[...], wi_ref[:, d_ff + c * ff_chunk:d_ff + (c + 1) * ff_chunk])
        act = (_silu(gt) * up).astype(BF16)
        part = _dot(act, wo_ref[cc, :])
        if c == 0:
            acc_s[...] = part
        else:
            acc_s[...] += part
    x2 = x1 + gate * acc_s[...]
    out_ref[...] = (x2 * _rms_scale(x2)) * fw_ref[...]


def _ffn(x1, mod3, wi, wo, fw_row, seq, ff_chunk):
    t, d = x1.shape
    tm = 1024
    per_b = seq // tm
    const = lambda shape: pl.BlockSpec(shape, lambda i: (0,) * len(shape),
                                       pipeline_mode=pl.Buffered(1))
    return pl.pallas_call(
        functools.partial(_ffn_kernel, ff_chunk=ff_chunk),
        grid=(t // tm,),
        in_specs=[pl.BlockSpec((tm, d), lambda i: (i, 0)),
                  pl.BlockSpec((1, 6, d), lambda i: (i // per_b, 0, 0)),
                  const(wi.shape), const(wo.shape), const((1, d))],
        out_specs=pl.BlockSpec((tm, d), lambda i: (i, 0)),
        out_shape=jax.ShapeDtypeStruct((t, d), F32),
        scratch_shapes=[pltpu.VMEM((tm, d), BF16), pltpu.VMEM((tm, d), F32)],
        compiler_params=pltpu.CompilerParams(
            dimension_semantics=("parallel",),
            vmem_limit_bytes=VMEM_LIMIT_BYTES),
        name="ffn",
    )(x1, mod3, wi, wo, fw_row)


def _pad_lanes(row, offset=0):
    return jnp.zeros((1, LANES), F32).at[0, offset:offset + row.shape[0]].set(row)


def kernel(x, c, w_ada, b_ada, w_in, ssd_conv_w, ssd_conv_b, ssd_dt_bias, ssd_a_log, ssd_d,
           ssd_norm_w, mlstm_conv_w, mlstm_conv_b, mlstm_if_bias, mlstm_norm_w,
           w_branch_ssd, w_branch_mlstm, w_out, w_ffn_in, w_ffn_out, final_norm_w):
    bsz, seq, d = x.shape
    depth = w_ada.shape[0]
    d_inner = ssd_norm_w.shape[-1]
    n_heads = ssd_dt_bias.shape[-1]
    gn = SSD_GROUPS * SSD_STATE
    d_qk = mlstm_conv_w.shape[-1] // 2
    d_v = mlstm_norm_w.shape[-1]
    d_ff = w_ffn_out.shape[1]
    assert ssd_conv_w.shape[-1] == d_inner + 2 * gn and d_inner == n_heads * SSD_HEADDIM
    assert n_heads + 2 * ML_HEADS <= LANES and seq % CHUNK == 0 and bsz % MIX_BATCH == 0
    assert d_inner == 2 * d and d_v == 2 * d and 2 * d_qk == 2 * d and gn * 2 == d

    sizes = (d_inner, d_inner + 2 * gn, n_heads, 2 * d_qk, d_v, d_v, 2 * ML_HEADS, 2 * d)
    offs = np.concatenate([[0], np.cumsum(sizes)])
    assert w_in.shape[-1] == offs[-1]
    col_idx = {"z": 0, "xs": 1, "qk": 2, "v": 3, "o": 4,
               "g_ssd": 10, "g_ml": 11, "bm": 24, "cm": 25}
    i_lane, f_lane = n_heads, n_heads + ML_HEADS
    ff_chunk = 256

    expand = (jnp.arange(2 * LANES)[:, None] % LANES
              == (jnp.arange(d_inner)[None, :] // SSD_HEADDIM)).astype(BF16)

    x2d = x.reshape(bsz * seq, d)
    for l in range(depth):
        wl = w_in[l]
        seg = lambda k: wl[:, offs[k]:offs[k + 1]]
        xbc_w = seg(1)
        w_main = jnp.concatenate(
            [seg(0), xbc_w[:, :d_inner], seg(3), seg(4), seg(5), seg(7), xbc_w[:, d_inner:]],
            axis=1).astype(BF16)
        w_small = jnp.concatenate(
            [seg(2), seg(6), jnp.zeros((d, LANES - n_heads - 2 * ML_HEADS), F32)],
            axis=1).astype(BF16)
        wi = w_ffn_in[l].astype(BF16)
        n_main = w_main.shape[1]
        conv_cols = ((d_inner, ssd_conv_w[l][:, :d_inner], ssd_conv_b[l][:d_inner]),
                     (2 * d_inner, mlstm_conv_w[l], mlstm_conv_b[l]),
                     (n_main - 2 * gn, ssd_conv_w[l][:, d_inner:], ssd_conv_b[l][d_inner:]))
        conv_w = jnp.zeros((CONV_K, n_main), F32)
        conv_b = jnp.zeros((1, n_main), F32)
        for off, cw, cb in conv_cols:
            conv_w = conv_w.at[:, off:off + cw.shape[1]].set(cw)
            conv_b = conv_b.at[0, off:off + cw.shape[1]].set(cb)
        tn = IN_PROJ_TN
        conv_tiles = tuple(sorted({t for off, cw, _ in conv_cols
                                   for t in range(off // tn, (off + cw.shape[1] - 1) // tn + 1)}))

        mod3 = _adaln(c, w_ada[l], b_ada[l]).reshape(bsz, 6, d)
        proj, small = _in_proj(x2d, mod3, w_main, w_small, conv_w, conv_b, seq, conv_tiles)
        proj3 = proj.reshape(bsz, seq, n_main)
        small3 = small.reshape(bsz, seq, LANES)
        y_ssd = _ssd(proj3, small3, _pad_lanes(ssd_dt_bias[l]), _pad_lanes(ssd_a_log[l]),
                     jnp.repeat(ssd_d[l], SSD_HEADDIM).reshape(1, d_inner),
                     ssd_norm_w[l].reshape(1, d_inner), expand, d_inner, col_idx
                     ).reshape(bsz * seq, d_inner)
        y_ml = _mlstm(proj3, small3, _pad_lanes(mlstm_if_bias[l], i_lane),
                      mlstm_norm_w[l].reshape(1, d_v), d_qk, d_v, col_idx, i_lane, f_lane
                      ).reshape(bsz * seq, d_v)
        x1 = _merge(y_ssd, y_ml, proj, x2d, mod3, w_branch_ssd[l].astype(BF16),
                    w_branch_mlstm[l].astype(BF16), w_out[l].astype(BF16), seq, col_idx)
        assert l == depth - 1 == 0
        x2d = _ffn(x1, mod3, wi, w_ffn_out[l].astype(BF16), final_norm_w.reshape(1, d), seq,
                   ff_chunk)
    return x2d.reshape(bsz, seq, d)
```

```python
import functools

import numpy as np
import jax
import jax.numpy as jnp
from jax import lax
from jax.experimental import pallas as pl
from jax.experimental.pallas import tpu as pltpu

F32 = jnp.float32
BF16 = jnp.bfloat16
HIGHEST = lax.Precision.HIGHEST

LANES = 128
SUBLANES = 8
VMEM_LIMIT_BYTES = 56 * 2**20

RMS_EPS = 1e-6
LOG2E = 1.4426950408889634
CHUNK = 128
CONV_K = 4
HIST = SUBLANES
IN_PROJ_TN = 1024
MIX_BATCH = 4
MIXER_FLAGS = None
SSD_GROUPS = 4
SSD_STATE = 128
SSD_HEADDIM = 64
ML_HEADS = 4

NT_DIMS = (((1,), (1,)), ((), ()))
TN_DIMS = (((0,), (0,)), ((), ()))


def _dot(a, b):
    return jnp.dot(a, b, preferred_element_type=F32)


def _sigmoid(x):
    return jax.nn.sigmoid(x)


def _silu(x):
    return x * _sigmoid(x)


def _softplus(x):
    return jnp.maximum(x, 0.0) + jnp.log1p(jnp.exp(-jnp.abs(x)))


def _rms_scale(x):
    return lax.rsqrt(jnp.mean(x * x, axis=-1, keepdims=True) + RMS_EPS)


def _split_bf16(v):
    hi = v.astype(BF16)
    lo = (v - hi.astype(F32)).astype(BF16)
    return hi, lo


def _tril_mask(n):
    row = lax.broadcasted_iota(jnp.int32, (n, n), 0)
    col = lax.broadcasted_iota(jnp.int32, (n, n), 1)
    return row >= col


def _adaln_kernel(c_ref, w_ref, b_ref, o_ref):
    c = c_ref[...]
    o_ref[...] = jnp.dot(_silu(c), w_ref[...], precision=HIGHEST,
                         preferred_element_type=F32) + b_ref[...]


def _adaln(c, w_ada, b_ada):
    bsz, d = c.shape
    n = w_ada.shape[1]
    tn = d
    return pl.pallas_call(
        _adaln_kernel,
        grid=(n // tn,),
        in_specs=[pl.BlockSpec((bsz, d), lambda j: (0, 0)),
                  pl.BlockSpec((d, tn), lambda j: (0, j)),
                  pl.BlockSpec((1, tn), lambda j: (0, j))],
        out_specs=pl.BlockSpec((bsz, tn), lambda j: (0, j)),
        out_shape=jax.ShapeDtypeStruct((bsz, n), F32),
        compiler_params=pltpu.CompilerParams(dimension_semantics=("parallel",)),
        name="adaln",
    )(c, w_ada, b_ada.reshape(1, n))


def _conv_silu(buf_ref, cw_ref, cb_ref, t, r0, rows):
    cols = slice(t * LANES, (t + 1) * LANES)
    acc = cb_ref[:, cols]
    for j in range(CONV_K):
        s = HIST - (CONV_K - 1) + j + r0
        acc = acc + cw_ref[j:j + 1, cols] * buf_ref[t, s:s + rows, :]
    return _silu(acc)


def _in_proj_kernel(x_ref, mod_ref, w_ref, ws_ref, cw_ref, cb_ref, proj_ref, small_ref, h_ref,
                    buf_ref, buf2_ref, *, norm_rows, row_block, conv_block, conv_rows, conv_tiles):
    j = pl.program_id(1)
    tm = x_ref.shape[0]
    tn = w_ref.shape[1]

    assert 0 not in conv_tiles

    @pl.when(j == 0)
    def _():
        shift = mod_ref[0, 0:1, :]
        scale = 1.0 + mod_ref[0, 1:2, :]
        for rb in range(0, tm, row_block):
            for r0 in range(rb, rb + row_block, norm_rows):
                x = x_ref[r0:r0 + norm_rows, :]
                hb = ((x * _rms_scale(x)) * scale + shift).astype(BF16)
                h_ref[r0:r0 + norm_rows, :] = hb
                small_ref[r0:r0 + norm_rows, :] = _dot(hb, ws_ref[...])
            proj_ref[rb:rb + row_block, :] = _dot(h_ref[rb:rb + row_block, :],
                                                  w_ref[...]).astype(BF16)

    is_conv = functools.reduce(jnp.logical_or, [j == t for t in conv_tiles])

    @pl.when(jnp.logical_not(is_conv) & (j != 0))
    def _():
        for rb in range(tm // row_block):
            rows = slice(rb * row_block, (rb + 1) * row_block)
            proj_ref[rows, :] = _dot(h_ref[rows, :], w_ref[...]).astype(BF16)

    @pl.when(is_conv)
    def _():
        nblk = tn // LANES
        bufs = (buf_ref, buf2_ref)
        for t in range(nblk):
            buf_ref[t, 0:HIST, :] = jnp.zeros((HIST, LANES), F32)
        for ib, rb in enumerate(range(0, tm, conv_block)):
            cur, nxt = bufs[ib % 2], bufs[(ib + 1) % 2]
            res = _dot(h_ref[rb:rb + conv_block, :], w_ref[...])
            for t in range(nblk):
                cur[t, HIST:HIST + conv_block, :] = res[:, t * LANES:(t + 1) * LANES]
            for t in range(nblk):
                nxt[t, 0:HIST, :] = res[conv_block - HIST:conv_block, t * LANES:(t + 1) * LANES]
            for t in range(nblk):
                for r0 in range(0, conv_block, conv_rows):
                    proj_ref[rb + r0:rb + r0 + conv_rows, t * LANES:(t + 1) * LANES] = _conv_silu(
                        cur, cw_ref, cb_ref, t, r0, conv_rows).astype(BF16)


def _in_proj(x2d, mod3, w_main, w_small, conv_w, conv_b, seq, conv_tiles):
    t, d = x2d.shape
    n = w_main.shape[1]
    tm, tn = seq, IN_PROJ_TN
    row_block, conv_block, conv_rows = 512, 512, 256
    return pl.pallas_call(
        functools.partial(_in_proj_kernel, norm_rows=256, row_block=row_block,
                          conv_block=conv_block, conv_rows=conv_rows, conv_tiles=conv_tiles),
        grid=(t // tm, n // tn),
        in_specs=[pl.BlockSpec((tm, d), lambda i, j: (i, 0)),
                  pl.BlockSpec((1, 6, d), lambda i, j: (i, 0, 0)),
                  pl.BlockSpec((d, tn), lambda i, j: (0, j)),
                  pl.BlockSpec((d, LANES), lambda i, j: (0, 0)),
                  pl.BlockSpec((CONV_K, tn), lambda i, j: (0, j)),
                  pl.BlockSpec((1, tn), lambda i, j: (0, j))],
        out_specs=[pl.BlockSpec((tm, tn), lambda i, j: (i, j)),
                   pl.BlockSpec((tm, LANES), lambda i, j: (i, 0))],
        out_shape=[jax.ShapeDtypeStruct((t, n), BF16),
                   jax.ShapeDtypeStruct((t, LANES), F32)],
        scratch_shapes=[pltpu.VMEM((tm, d), BF16),
                        pltpu.VMEM((tn // LANES, HIST + conv_block, LANES), F32),
                        pltpu.VMEM((tn // LANES, HIST + conv_block, LANES), F32)],
        compiler_params=pltpu.CompilerParams(
            dimension_semantics=("parallel", "arbitrary"),
            vmem_limit_bytes=VMEM_LIMIT_BYTES),
        name="in_proj",
    )(x2d, mod3, w_main, w_small, conv_w, conv_b)


def _ssd_kernel(z_ref, xs_ref, bm_ref, cm_ref, sm_ref, dtb_ref, alog_ref, dexp_ref,
                e_ref, out_ref, st_ref, y_s):
    @pl.when(pl.program_id(1) == 0)
    def _():
        st_ref[...] = jnp.zeros(st_ref.shape, F32)

    nb = z_ref.shape[0]
    L = CHUNK
    G, N, P = SSD_GROUPS, SSD_STATE, SSD_HEADDIM
    d_inner = xs_ref.shape[2]
    gw = d_inner // G
    n_heads = d_inner // P
    heads_per_tile = LANES // P
    tril = _tril_mask(L)
    tril_f = tril.astype(F32)
    lane = lax.broadcasted_iota(jnp.int32, (L, LANES), 1)
    lane_row = lax.broadcasted_iota(jnp.int32, (1, LANES), 1)
    head_mask = [((lane_row >= q * P) & (lane_row < (q + 1) * P)).astype(BF16)
                 for q in range(heads_per_tile)]
    insts = [(s, g) for s in range(nb) for g in range(G)]

    sq = []
    for s in range(nb):
        dt = jnp.where(lane < n_heads, _softplus(sm_ref[s] + dtb_ref[...]), 0.0)
        a = dt * (-jnp.exp(alog_ref[...]))
        acs = jnp.dot(tril_f, a, precision=HIGHEST, preferred_element_type=F32)
        acs2 = acs * LOG2E
        a_tot = acs[L - 1:L, :]
        w_dt = jnp.exp(a_tot - acs) * dt
        e_tot = jnp.broadcast_to(jnp.exp(a_tot), (SUBLANES, LANES))
        stacked = jnp.concatenate([dt, jnp.exp2(acs2), w_dt, e_tot], axis=0)
        sq.append(dict(acs2=acs2, acs2_t=acs2.T,
                       s_cat=jnp.concatenate(_split_bf16(stacked), axis=1)))

    st = {}
    for s, g in insts:
        gcols = slice(g * gw, (g + 1) * gw)
        ex = _dot(sq[s]["s_cat"], e_ref[:, gcols])
        xs_g = xs_ref[s, :, gcols].astype(F32)
        bm_g = bm_ref[s, :, g * N:(g + 1) * N]
        cm_g = cm_ref[s, :, g * N:(g + 1) * N]
        st[s, g] = dict(
            eacs_e=ex[L:2 * L], etot_e=ex[3 * L:3 * L + 1], xs_g=xs_g,
            xdt_g=(xs_g * ex[0:L]).astype(BF16), xw_g=(xs_g * ex[2 * L:3 * L]).astype(BF16),
            cb_g=lax.dot_general(cm_g, bm_g, NT_DIMS, preferred_element_type=F32))

    for s, g in insts:
        e = st[s, g]
        bm_g = bm_ref[s, :, g * N:(g + 1) * N]
        cm_g = cm_ref[s, :, g * N:(g + 1) * N]
        st_g = st_ref[s, g]
        e["y_off"] = _dot(cm_g, st_g.astype(BF16)) * e["eacs_e"]
        st_ref[s, g] = st_g * e["etot_e"] + lax.dot_general(
            bm_g, e["xw_g"], TN_DIMS, preferred_element_type=F32)

    ssq = [jnp.zeros((L, LANES), F32) for _ in range(nb)]
    tiles = [(s, g, p) for s, g in insts for p in range(gw // LANES)]
    y_diag = {}
    for s, g, p in tiles:
        e = st[s, g]
        acs2, acs2_t = sq[s]["acs2"], sq[s]["acs2_t"]
        pc = slice(p * LANES, (p + 1) * LANES)
        ms = []
        for q in range(heads_per_tile):
            h = (g * gw + p * LANES) // P + q
            seg = acs2[:, h:h + 1] - acs2_t[h:h + 1, :]
            dec = jnp.exp2(jnp.where(tril, seg, -jnp.inf))
            ms.append((e["cb_g"] * dec).astype(BF16))
        lhs = jnp.concatenate(ms, axis=1)
        xp = e["xdt_g"][:, pc]
        rhs = jnp.concatenate([xp * m for m in head_mask], axis=0)
        y_diag[s, g, p] = _dot(lhs, rhs)
    for s, g, p in tiles:
        e = st[s, g]
        pc = slice(p * LANES, (p + 1) * LANES)
        col = slice(g * gw + p * LANES, g * gw + (p + 1) * LANES)
        y = y_diag[s, g, p] + e["y_off"][:, pc] + e["xs_g"][:, pc] * dexp_ref[:, col]
        gated = y * _silu(z_ref[s, :, col].astype(F32))
        y_s[s, :, col] = gated
        ssq[s] = ssq[s] + gated * gated

    for s in range(nb):
        inv = lax.rsqrt(jnp.sum(ssq[s], axis=-1, keepdims=True) / d_inner + RMS_EPS)
        out_ref[s] = (y_s[s] * inv).astype(BF16)


def _ssd(proj, small, dtb_row, alog_row, dexp_row, expand, d_inner, col_idx):
    bsz, seq, _ = proj.shape
    L, nb = CHUNK, MIX_BATCH
    gn = SSD_GROUPS * SSD_STATE
    blk = lambda width, col: pl.BlockSpec((nb, L, width), lambda b, c: (b, c, col))
    full = lambda shape: pl.BlockSpec(shape, lambda b, c: (0,) * len(shape))
    return pl.pallas_call(
        _ssd_kernel,
        grid=(bsz // nb, seq // L),
        in_specs=[blk(d_inner, col_idx["z"]), blk(d_inner, col_idx["xs"]),
                  blk(gn, col_idx["bm"]), blk(gn, col_idx["cm"]), blk(LANES, 0),
                  full((1, LANES)), full((1, LANES)),
                  full((1, d_inner)),
                  full((2 * LANES, d_inner))],
        out_specs=blk(d_inner, 0),
        out_shape=jax.ShapeDtypeStruct((bsz, seq, d_inner), BF16),
        scratch_shapes=[pltpu.VMEM((nb, SSD_GROUPS, SSD_STATE, d_inner // SSD_GROUPS), F32),
                        pltpu.VMEM((nb, L, d_inner), F32)],
        compiler_params=pltpu.CompilerParams(
            dimension_semantics=("parallel", "arbitrary"),
            vmem_limit_bytes=VMEM_LIMIT_BYTES, flags=MIXER_FLAGS),
        name="ssd",
    )(proj, proj, proj, proj, small, dtb_row, alog_row, dexp_row, expand)


def _mlstm_kernel(qk_ref, v_ref, o_ref, sm_ref, ifb_ref, nw_ref, out_ref, c_ref, n_ref, m_ref,
                  *, i_lane, f_lane):
    @pl.when(pl.program_id(1) == 0)
    def _():
        c_ref[...] = jnp.zeros(c_ref.shape, F32)
        n_ref[...] = jnp.zeros(n_ref.shape, F32)
        m_ref[...] = jnp.zeros(m_ref.shape, F32)

    nb = qk_ref.shape[0]
    L = CHUNK
    H = ML_HEADS
    d_qk = qk_ref.shape[2] // 2
    d_v = v_ref.shape[2]
    dk, dv = d_qk // H, d_v // H
    q_scale = dk ** -0.5
    assert 2.0 ** round(np.log2(q_scale)) == q_scale
    tril = _tril_mask(L)
    tril_f = tril.astype(F32)
    insts = [(s, h) for s in range(nb) for h in range(H)]

    seq_gates = []
    for s in range(nb):
        gates = sm_ref[s] + ifb_ref[...]
        log_f = -_softplus(-gates)
        b_cum = jnp.dot(tril_f, log_f, precision=HIGHEST, preferred_element_type=F32)
        seq_gates.append((gates, gates.T, b_cum, b_cum.T))

    st = {}
    for s, h in insts:
        gates, gates_t, b_cum, b_cum_t = seq_gates[s]
        li_col = gates[:, i_lane + h:i_lane + h + 1]
        li_row = gates_t[i_lane + h:i_lane + h + 1, :]
        b_col = b_cum[:, f_lane + h:f_lane + h + 1]
        b_row = b_cum_t[f_lane + h:f_lane + h + 1, :]
        g_tot = b_col[L - 1:L, :]
        m_prev = m_ref[s, h, 0:1, 0:1]
        d = jnp.where(tril, (b_col - b_row) + li_row, -jnp.inf)
        m_inter = b_col + m_prev
        m_t = jnp.maximum(m_inter, jnp.max(d, axis=-1, keepdims=True))
        a_loc = (g_tot - b_col) + li_col
        m_loc = jnp.max(a_loc, axis=0, keepdims=True)
        m_new = jnp.maximum(g_tot + m_prev, m_loc)
        st[s, h] = dict(d=d, m_inter=m_inter, m_t=m_t, a_loc=a_loc, m_new=m_new,
                        sp=jnp.exp(g_tot + m_prev - m_new))

    for s, h in insts:
        e = st[s, h]
        qb = qk_ref[s, :, h * dk:(h + 1) * dk] * q_scale
        kb = qk_ref[s, :, d_qk + h * dk:d_qk + (h + 1) * dk]
        e["scores"] = (lax.dot_general(qb, kb, NT_DIMS, preferred_element_type=F32)
                       * jnp.exp(e["d"] - e["m_t"]))
        e["w_inter"] = jnp.exp(e["m_inter"] - e["m_t"])

    for s, h in insts:
        e = st[s, h]
        q = (qk_ref[s, :, h * dk:(h + 1) * dk] * q_scale).astype(F32)
        vb = v_ref[s, :, h * dv:(h + 1) * dv]
        lhs = jnp.concatenate([e["scores"].astype(BF16), (q * e["w_inter"]).astype(BF16)], axis=1)
        rhs = jnp.concatenate([vb, c_ref[s, h].astype(BF16)], axis=0)
        e["num"] = _dot(lhs, rhs)
        den = (jnp.sum(e["scores"], axis=-1, keepdims=True)
               + e["w_inter"] * jnp.sum(q * n_ref[s, h, 0:1, :], axis=-1, keepdims=True))
        e["den"] = jnp.maximum(jnp.abs(den), jnp.exp(-e["m_t"]))

    for s, h in insts:
        e = st[s, h]
        k = qk_ref[s, :, d_qk + h * dk:d_qk + (h + 1) * dk].astype(F32)
        vb = v_ref[s, :, h * dv:(h + 1) * dv]
        kw = k * jnp.exp(e["a_loc"] - e["m_new"])
        c_ref[s, h] = e["sp"] * c_ref[s, h] + lax.dot_general(
            kw.astype(BF16), vb, TN_DIMS, preferred_element_type=F32)
        n_ref[s, h, 0:1, :] = e["sp"] * n_ref[s, h, 0:1, :] + jnp.sum(kw, axis=0, keepdims=True)
        m_ref[s, h] = jnp.broadcast_to(e["m_new"], m_ref.shape[2:])

    for s, h in insts:
        e = st[s, h]
        vcols = slice(h * dv, (h + 1) * dv)
        inv_den = 1.0 / e["den"]
        msq = jnp.mean(e["num"] * e["num"], axis=-1, keepdims=True)
        row_scale = inv_den * lax.rsqrt(inv_den * inv_den * msq + RMS_EPS)
        gate = 0.5 * jnp.tanh(0.5 * o_ref[s, :, vcols]) + 0.5
        out_ref[s, :, vcols] = gate * (e["num"] * row_scale * nw_ref[:, vcols]).astype(BF16)


def _mlstm(proj, small, ifb_row, nw_row, d_qk, d_v, col_idx, i_lane, f_lane):
    bsz, seq, _ = proj.shape
    L, nb = CHUNK, MIX_BATCH
    H = ML_HEADS
    blk = lambda width, col: pl.BlockSpec((nb, L, width), lambda b, c: (b, c, col))
    full = lambda shape: pl.BlockSpec(shape, lambda b, c: (0,) * len(shape))
    return pl.pallas_call(
        functools.partial(_mlstm_kernel, i_lane=i_lane, f_lane=f_lane),
        grid=(bsz // nb, seq // L),
        in_specs=[blk(2 * d_qk, col_idx["qk"]), blk(d_v, col_idx["v"]), blk(d_v, col_idx["o"]),
                  blk(LANES, 0), full((1, LANES)), full((1, d_v))],
        out_specs=blk(d_v, 0),
        out_shape=jax.ShapeDtypeStruct((bsz, seq, d_v), BF16),
        scratch_shapes=[pltpu.VMEM((nb, H, d_qk // H, d_v // H), F32),
                        pltpu.VMEM((nb, H, SUBLANES, d_qk // H), F32),
                        pltpu.VMEM((nb, H, SUBLANES, LANES), F32)],
        compiler_params=pltpu.CompilerParams(
            dimension_semantics=("parallel", "arbitrary"),
            vmem_limit_bytes=VMEM_LIMIT_BYTES, flags=MIXER_FLAGS),
        name="mlstm",
    )(proj, proj, proj, small, ifb_row, nw_row)


def _merge_kernel(ys_ref, ym_ref, gs_ref, gm_ref, x_ref, mod_ref, wbs_ref, wbm_ref, wo_ref,
                  out_ref):
    a = _dot(ys_ref[...], wbs_ref[...])
    b = _dot(ym_ref[...], wbm_ref[...])
    merged = (_sigmoid(gs_ref[...].astype(F32)) * a + _sigmoid(gm_ref[...].astype(F32)) * b)
    o = _dot(merged.astype(BF16), wo_ref[...])
    out_ref[...] = x_ref[...] + mod_ref[0, 2:3, :] * o


def _merge(y_ssd, y_ml, proj, x2d, mod3, wbs, wbm, wo, seq, col_idx):
    t, d = x2d.shape
    dv = y_ssd.shape[1]
    tm = 512
    per_b = seq // tm
    const = lambda shape: pl.BlockSpec(shape, lambda i: (0,) * len(shape),
                                       pipeline_mode=pl.Buffered(1))
    return pl.pallas_call(
        _merge_kernel,
        grid=(t // tm,),
        in_specs=[pl.BlockSpec((tm, dv), lambda i: (i, 0)),
                  pl.BlockSpec((tm, dv), lambda i: (i, 0)),
                  pl.BlockSpec((tm, d), lambda i: (i, col_idx["g_ssd"])),
                  pl.BlockSpec((tm, d), lambda i: (i, col_idx["g_ml"])),
                  pl.BlockSpec((tm, d), lambda i: (i, 0)),
                  pl.BlockSpec((1, 6, d), lambda i: (i // per_b, 0, 0)),
                  const((dv, d)), const((dv, d)), const((d, d))],
        out_specs=pl.BlockSpec((tm, d), lambda i: (i, 0)),
        out_shape=jax.ShapeDtypeStruct((t, d), F32),
        compiler_params=pltpu.CompilerParams(
            dimension_semantics=("parallel",),
            vmem_limit_bytes=VMEM_LIMIT_BYTES),
        name="merge",
    )(y_ssd, y_ml, proj, proj, x2d, mod3, wbs, wbm, wo)


def _ffn_kernel(x_ref, mod_ref, wi_ref, wo_ref, fw_ref, out_ref, h_s, acc_s, *, ff_chunk):
    x1 = x_ref[...]
    shift = mod_ref[0, 3:4, :]
    scale = 1.0 + mod_ref[0, 4:5, :]
    gate = mod_ref[0, 5:6, :]
    h_s[...] = ((x1 * _rms_scale(x1)) * scale + shift).astype(BF16)
    d_ff = wo_ref.shape[0]
    for c in range(d_ff // ff_chunk):
        cc = slice(c * ff_chunk, (c + 1) * ff_chunk)
        gt = _dot(h_s[...], wi_ref[:, cc])
        up = _dot(h_s[...], wi_ref[:, d_ff + c * ff_chunk:d_ff + (c + 1) * ff_chunk])
        act = (_silu(gt) * up).astype(BF16)
        part = _dot(act, wo_ref[cc, :])
        if c == 0:
            acc_s[...] = part
        else:
            acc_s[...] += part
    x2 = x1 + gate * acc_s[...]
    out_ref[...] = (x2 * _rms_scale(x2)) * fw_ref[...]


def _ffn(x1, mod3, wi, wo, fw_row, seq, ff_chunk):
    t, d = x1.shape
    tm = 1024
    per_b = seq // tm
    const = lambda shape: pl.BlockSpec(shape, lambda i: (0,) * len(shape),
                                       pipeline_mode=pl.Buffered(1))
    return pl.pallas_call(
        functools.partial(_ffn_kernel, ff_chunk=ff_chunk),
        grid=(t // tm,),
        in_specs=[pl.BlockSpec((tm, d), lambda i: (i, 0)),
                  pl.BlockSpec((1, 6, d), lambda i: (i // per_b, 0, 0)),
                  const(wi.shape), const(wo.shape), const((1, d))],
        out_specs=pl.BlockSpec((tm, d), lambda i: (i, 0)),
        out_shape=jax.ShapeDtypeStruct((t, d), F32),
        scratch_shapes=[pltpu.VMEM((tm, d), BF16), pltpu.VMEM((tm, d), F32)],
        compiler_params=pltpu.CompilerParams(
            dimension_semantics=("parallel",),
            vmem_limit_bytes=VMEM_LIMIT_BYTES),
        name="ffn",
    )(x1, mod3, wi, wo, fw_row)


def _pad_lanes(row, offset=0):
    return jnp.zeros((1, LANES), F32).at[0, offset:offset + row.shape[0]].set(row)


def kernel(x, c, w_ada, b_ada, w_in, ssd_conv_w, ssd_conv_b, ssd_dt_bias, ssd_a_log, ssd_d,
           ssd_norm_w, mlstm_conv_w, mlstm_conv_b, mlstm_if_bias, mlstm_norm_w,
           w_branch_ssd, w_branch_mlstm, w_out, w_ffn_in, w_ffn_out, final_norm_w):
    bsz, seq, d = x.shape
    depth = w_ada.shape[0]
    d_inner = ssd_norm_w.shape[-1]
    n_heads = ssd_dt_bias.shape[-1]
    gn = SSD_GROUPS * SSD_STATE
    d_qk = mlstm_conv_w.shape[-1] // 2
    d_v = mlstm_norm_w.shape[-1]
    d_ff = w_ffn_out.shape[1]
    assert ssd_conv_w.shape[-1] == d_inner + 2 * gn and d_inner == n_heads * SSD_HEADDIM
    assert n_heads + 2 * ML_HEADS <= LANES and seq % CHUNK == 0 and bsz % MIX_BATCH == 0
    assert d_inner == 2 * d and d_v == 2 * d and 2 * d_qk == 2 * d and gn * 2 == d

    sizes = (d_inner, d_inner + 2 * gn, n_heads, 2 * d_qk, d_v, d_v, 2 * ML_HEADS, 2 * d)
    offs = np.concatenate([[0], np.cumsum(sizes)])
    assert w_in.shape[-1] == offs[-1]
    col_idx = {"z": 0, "xs": 1, "qk": 2, "v": 3, "o": 4,
               "g_ssd": 10, "g_ml": 11, "bm": 24, "cm": 25}
    i_lane, f_lane = n_heads, n_heads + ML_HEADS
    ff_chunk = 256

    expand = (jnp.arange(2 * LANES)[:, None] % LANES
              == (jnp.arange(d_inner)[None, :] // SSD_HEADDIM)).astype(BF16)

    x2d = x.reshape(bsz * seq, d)
    for l in range(depth):
        wl = w_in[l]
        cut = lambda a, b: wl[:, a:b].astype(BF16)
        w_main = jnp.concatenate(
            [cut(offs[0], offs[1] + d_inner),
             cut(offs[3], offs[6]),
             cut(offs[7], offs[8]),
             cut(offs[1] + d_inner, offs[2])],
            axis=1)
        w_small = jnp.concatenate(
            [cut(offs[2], offs[3]), cut(offs[6], offs[7]),
             jnp.zeros((d, LANES - n_heads - 2 * ML_HEADS), BF16)], axis=1)
        wi = w_ffn_in[l].astype(BF16)
        n_main = w_main.shape[1]
        conv_cols = ((d_inner, ssd_conv_w[l][:, :d_inner], ssd_conv_b[l][:d_inner]),
                     (2 * d_inner, mlstm_conv_w[l], mlstm_conv_b[l]),
                     (n_main - 2 * gn, ssd_conv_w[l][:, d_inner:], ssd_conv_b[l][d_inner:]))
        conv_w = jnp.zeros((CONV_K, n_main), F32)
        conv_b = jnp.zeros((1, n_main), F32)
        for off, cw, cb in conv_cols:
            conv_w = conv_w.at[:, off:off + cw.shape[1]].set(cw)
            conv_b = conv_b.at[0, off:off + cw.shape[1]].set(cb)
        tn = IN_PROJ_TN
        conv_tiles = tuple(sorted({t for off, cw, _ in conv_cols
                                   for t in range(off // tn, (off + cw.shape[1] - 1) // tn + 1)}))

        mod3 = _adaln(c, w_ada[l], b_ada[l]).reshape(bsz, 6, d)
        proj, small = _in_proj(x2d, mod3, w_main, w_small, conv_w, conv_b, seq, conv_tiles)
        proj3 = proj.reshape(bsz, seq, n_main)
        small3 = small.reshape(bsz, seq, LANES)
        y_ssd = _ssd(proj3, small3, _pad_lanes(ssd_dt_bias[l]), _pad_lanes(ssd_a_log[l]),
                     jnp.repeat(ssd_d[l], SSD_HEADDIM).reshape(1, d_inner), expand, d_inner,
                     col_idx).reshape(bsz * seq, d_inner)
        y_ml = _mlstm(proj3, small3, _pad_lanes(mlstm_if_bias[l], i_lane),
                      mlstm_norm_w[l].reshape(1, d_v), d_qk, d_v, col_idx, i_lane, f_lane
                      ).reshape(bsz * seq, d_v)
        wbs = (ssd_norm_w[l][:, None] * w_branch_ssd[l]).astype(BF16)
        x1 = _merge(y_ssd, y_ml, proj, x2d, mod3, wbs, w_branch_mlstm[l].astype(BF16),
                    w_out[l].astype(BF16), seq, col_idx)
        assert l == depth - 1 == 0
        x2d = _ffn(x1, mod3, wi, w_ffn_out[l].astype(BF16), final_norm_w.reshape(1, d), seq,
                   ff_chunk)
    return x2d.reshape(bsz, seq, d)
```

```python
import functools

import numpy as np
import jax
import jax.numpy as jnp
from jax import lax
from jax.experimental import pallas as pl
from jax.experimental.pallas import tpu as pltpu

F32 = jnp.float32
BF16 = jnp.bfloat16
HIGHEST = lax.Precision.HIGHEST

LANES = 128
SUBLANES = 8
VMEM_LIMIT_BYTES = 56 * 2**20

RMS_EPS = 1e-6
LOG2E = 1.4426950408889634
CHUNK = 128
CONV_K = 4
HIST = SUBLANES
IN_PROJ_TN = 1024
MIX_BATCH = 4
SSD_BATCH = MIX_BATCH
MIX_GROUP = MIX_BATCH
MIXER_FLAGS = None
SSD_GROUPS = 4
SSD_STATE = 128
SSD_HEADDIM = 64
ML_HEADS = 4

NT_DIMS = (((1,), (1,)), ((), ()))
TN_DIMS = (((0,), (0,)), ((), ()))


def _dot(a, b):
    return jnp.dot(a, b, preferred_element_type=F32)


def _sigmoid(x):
    return jax.nn.sigmoid(x)


def _silu(x):
    return x * _sigmoid(x)


def _softplus(x):
    return jnp.maximum(x, 0.0) + jnp.log1p(jnp.exp(-jnp.abs(x)))


def _rms_scale(x):
    return lax.rsqrt(jnp.mean(x * x, axis=-1, keepdims=True) + RMS_EPS)


def _split_bf16(v):
    hi = v.astype(BF16)
    lo = (v - hi.astype(F32)).astype(BF16)
    return hi, lo


def _tril_mask(n):
    row = lax.broadcasted_iota(jnp.int32, (n, n), 0)
    col = lax.broadcasted_iota(jnp.int32, (n, n), 1)
    return row >= col


def _cumsum_rows(x):
    n = x.shape[0]
    row = lax.broadcasted_iota(jnp.int32, x.shape, 0)
    k = 1
    while k < n:
        x = x + jnp.where(row >= k, pltpu.roll(x, k, 0), 0.0)
        k *= 2
    return x


def _adaln_kernel(c_ref, w_ref, b_ref, o_ref):
    c = c_ref[...]
    o_ref[...] = jnp.dot(_silu(c), w_ref[...], precision=HIGHEST,
                         preferred_element_type=F32) + b_ref[...]


def _adaln(c, w_ada, b_ada):
    bsz, d = c.shape
    n = w_ada.shape[1]
    tn = d
    return pl.pallas_call(
        _adaln_kernel,
        grid=(n // tn,),
        in_specs=[pl.BlockSpec((bsz, d), lambda j: (0, 0)),
                  pl.BlockSpec((d, tn), lambda j: (0, j)),
                  pl.BlockSpec((1, tn), lambda j: (0, j))],
        out_specs=pl.BlockSpec((bsz, tn), lambda j: (0, j)),
        out_shape=jax.ShapeDtypeStruct((bsz, n), F32),
        compiler_params=pltpu.CompilerParams(dimension_semantics=("parallel",)),
        name="adaln",
    )(c, w_ada, b_ada.reshape(1, n))


def _regroup_kernel(starts_ref, w_ref, o_ref, *, shifts):
    j = pl.program_id(0)
    tn = o_ref.shape[1]
    for s in sorted(set(shifts)):
        sel = functools.reduce(jnp.logical_or, [j == t for t, sh in enumerate(shifts) if sh == s])

        @pl.when(sel)
        def _():
            o_ref[...] = w_ref[:, s:s + tn].astype(BF16)


def _regroup_weights(w, src_cols, tn):
    d = w.shape[0]
    starts = np.array([c // LANES for c in src_cols], np.int32)
    shifts = tuple(int(c - s * LANES) for c, s in zip(src_cols, starts))
    grid_spec = pltpu.PrefetchScalarGridSpec(
        num_scalar_prefetch=1, grid=(len(src_cols),),
        in_specs=[pl.BlockSpec((pl.Element(d), pl.Element(tn + LANES, padding=(0, LANES))),
                               lambda j, st: (0, st[j] * LANES))],
        out_specs=pl.BlockSpec((d, tn), lambda j, st: (0, j)))
    return pl.pallas_call(
        functools.partial(_regroup_kernel, shifts=shifts), grid_spec=grid_spec,
        out_shape=jax.ShapeDtypeStruct((d, tn * len(src_cols)), BF16),
        compiler_params=pltpu.CompilerParams(dimension_semantics=("arbitrary",)),
        name="regroup")(jnp.asarray(starts), w)


def _conv_silu(buf_ref, cw_ref, cb_ref, t, r0, rows):
    cols = slice(t * LANES, (t + 1) * LANES)
    acc = cb_ref[:, cols]
    for j in range(CONV_K):
        s = HIST - (CONV_K - 1) + j + r0
        acc = acc + cw_ref[j:j + 1, cols] * buf_ref[t, s:s + rows, :]
    return _silu(acc)


def _in_proj_kernel(x_ref, mod_ref, w_ref, ws_ref, cw_ref, cb_ref, proj_ref, small_ref, h_ref,
                    buf_ref, buf2_ref, *, norm_rows, row_block, conv_block, conv_rows, conv_tiles):
    j = pl.program_id(1)
    tm = x_ref.shape[0]
    tn = w_ref.shape[1]

    assert 0 not in conv_tiles

    @pl.when(j == 0)
    def _():
        shift = mod_ref[0, 0:1, :]
        scale = 1.0 + mod_ref[0, 1:2, :]
        for rb in range(0, tm, row_block):
            for r0 in range(rb, rb + row_block, norm_rows):
                x = x_ref[r0:r0 + norm_rows, :]
                hb = ((x * _rms_scale(x)) * scale + shift).astype(BF16)
                h_ref[r0:r0 + norm_rows, :] = hb
                small_ref[r0:r0 + norm_rows, :] = _dot(hb, ws_ref[...])
            proj_ref[rb:rb + row_block, :] = _dot(h_ref[rb:rb + row_block, :],
                                                  w_ref[...]).astype(BF16)

    is_conv = functools.reduce(jnp.logical_or, [j == t for t in conv_tiles])

    @pl.when(jnp.logical_not(is_conv) & (j != 0))
    def _():
        for rb in range(tm // row_block):
            rows = slice(rb * row_block, (rb + 1) * row_block)
            proj_ref[rows, :] = _dot(h_ref[rows, :], w_ref[...]).astype(BF16)

    @pl.when(is_conv)
    def _():
        nblk = tn // LANES
        bufs = (buf_ref, buf2_ref)
        for t in range(nblk):
            buf_ref[t, 0:HIST, :] = jnp.zeros((HIST, LANES), F32)
        n_blocks = tm // conv_block

        def matmul(ib):
            return _dot(h_ref[ib * conv_block:(ib + 1) * conv_block, :], w_ref[...])

        def stash(ib, res):
            cur, nxt = bufs[ib % 2], bufs[(ib + 1) % 2]
            for t in range(nblk):
                cur[t, HIST:HIST + conv_block, :] = res[:, t * LANES:(t + 1) * LANES]
            for t in range(nblk):
                nxt[t, 0:HIST, :] = res[conv_block - HIST:conv_block, t * LANES:(t + 1) * LANES]

        def epilogue(ib):
            rb = ib * conv_block
            for t in range(nblk):
                for r0 in range(0, conv_block, conv_rows):
                    proj_ref[rb + r0:rb + r0 + conv_rows, t * LANES:(t + 1) * LANES] = _conv_silu(
                        bufs[ib % 2], cw_ref, cb_ref, t, r0, conv_rows).astype(BF16)

        stash(0, matmul(0))
        for ib in range(1, n_blocks):
            res = matmul(ib)
            epilogue(ib - 1)
            stash(ib, res)
        epilogue(n_blocks - 1)


def _in_proj(x2d, mod3, w_main, w_small, conv_w, conv_b, seq, conv_tiles):
    t, d = x2d.shape
    n = w_main.shape[1]
    tm, tn = seq, IN_PROJ_TN
    row_block, conv_block, conv_rows = 512, 512, 256
    return pl.pallas_call(
        functools.partial(_in_proj_kernel, norm_rows=256, row_block=row_block,
                          conv_block=conv_block, conv_rows=conv_rows, conv_tiles=conv_tiles),
        grid=(t // tm, n // tn),
        in_specs=[pl.BlockSpec((tm, d), lambda i, j: (i, 0)),
                  pl.BlockSpec((1, 6, d), lambda i, j: (i, 0, 0)),
                  pl.BlockSpec((d, tn), lambda i, j: (0, j)),
                  pl.BlockSpec((d, LANES), lambda i, j: (0, 0)),
                  pl.BlockSpec((CONV_K, tn), lambda i, j: (0, j)),
                  pl.BlockSpec((1, tn), lambda i, j: (0, j))],
        out_specs=[pl.BlockSpec((tm, tn), lambda i, j: (i, j)),
                   pl.BlockSpec((tm, LANES), lambda i, j: (i, 0))],
        out_shape=[jax.ShapeDtypeStruct((t, n), BF16),
                   jax.ShapeDtypeStruct((t, LANES), F32)],
        scratch_shapes=[pltpu.VMEM((tm, d), BF16),
                        pltpu.VMEM((tn // LANES, HIST + conv_block, LANES), F32),
                        pltpu.VMEM((tn // LANES, HIST + conv_block, LANES), F32)],
        compiler_params=pltpu.CompilerParams(
            dimension_semantics=("parallel", "arbitrary"),
            vmem_limit_bytes=VMEM_LIMIT_BYTES),
        name="in_proj",
    )(x2d, mod3, w_main, w_small, conv_w, conv_b)


def _ssd_kernel(z_ref, xs_ref, bm_ref, cm_ref, sm_ref, dtb_ref, alog_ref, dexp_ref,
                e_ref, out_ref, st_ref, y_s, *, group):
    @pl.when(pl.program_id(1) == 0)
    def _():
        st_ref[...] = jnp.zeros(st_ref.shape, F32)

    nb = z_ref.shape[0]
    L = CHUNK
    G, N, P = SSD_GROUPS, SSD_STATE, SSD_HEADDIM
    d_inner = xs_ref.shape[2]
    gw = d_inner // G
    n_heads = d_inner // P
    heads_per_tile = LANES // P
    tril = _tril_mask(L)
    tril_f = tril.astype(F32)
    lane = lax.broadcasted_iota(jnp.int32, (L, LANES), 1)
    lane_row = lax.broadcasted_iota(jnp.int32, (1, LANES), 1)
    head_mask = [((lane_row >= q * P) & (lane_row < (q + 1) * P)).astype(BF16)
                 for q in range(heads_per_tile)]
    sq, st = {}, {}
    ssq = {s: jnp.zeros((L, LANES), F32) for s in range(nb)}

    def stage0(s):
        dt = jnp.where(lane < n_heads, _softplus(sm_ref[s] + dtb_ref[...]), 0.0)
        a = dt * (-jnp.exp(alog_ref[...]))
        acs = jnp.dot(tril_f, a, precision=HIGHEST, preferred_element_type=F32)
        acs2 = acs * LOG2E
        a_tot = acs[L - 1:L, :]
        w_dt = jnp.exp(a_tot - acs) * dt
        e_tot = jnp.broadcast_to(jnp.exp(a_tot), (SUBLANES, LANES))
        stacked = jnp.concatenate([dt, jnp.exp2(acs2), w_dt, e_tot], axis=0)
        sq[s] = dict(acs2=acs2, acs2_t=acs2.T,
                     s_cat=jnp.concatenate(_split_bf16(stacked), axis=1))

    def stage1(s, g):
        gcols = slice(g * gw, (g + 1) * gw)
        ex = _dot(sq[s]["s_cat"], e_ref[:, gcols])
        xs_g = xs_ref[s, :, gcols].astype(F32)
        bm_g = bm_ref[s, :, g * N:(g + 1) * N]
        cm_g = cm_ref[s, :, g * N:(g + 1) * N]
        st[s, g] = dict(
            eacs_e=ex[L:2 * L], etot_e=ex[3 * L:3 * L + 1], xs_g=xs_g,
            xdt_g=(xs_g * ex[0:L]).astype(BF16), xw_g=(xs_g * ex[2 * L:3 * L]).astype(BF16),
            cb_g=lax.dot_general(cm_g, bm_g, NT_DIMS, preferred_element_type=F32))

    def stage2(s, g):
        e = st[s, g]
        bm_g = bm_ref[s, :, g * N:(g + 1) * N]
        cm_g = cm_ref[s, :, g * N:(g + 1) * N]
        st_g = st_ref[s, g]
        e["y_off"] = _dot(cm_g, st_g.astype(BF16)) * e["eacs_e"]
        st_ref[s, g] = st_g * e["etot_e"] + lax.dot_general(
            bm_g, e["xw_g"], TN_DIMS, preferred_element_type=F32)

    def stage3(s, g):
        e = st[s, g]
        acs2, acs2_t = sq[s]["acs2"], sq[s]["acs2_t"]
        for p in range(gw // LANES):
            pc = slice(p * LANES, (p + 1) * LANES)
            ms = []
            for q in range(heads_per_tile):
                h = (g * gw + p * LANES) // P + q
                seg = acs2[:, h:h + 1] - acs2_t[h:h + 1, :]
                dec = jnp.exp2(jnp.where(tril, seg, -jnp.inf))
                ms.append((e["cb_g"] * dec).astype(BF16))
            lhs = jnp.concatenate(ms, axis=1)
            xp = e["xdt_g"][:, pc]
            rhs = jnp.concatenate([xp * m for m in head_mask], axis=0)
            col = slice(g * gw + p * LANES, g * gw + (p + 1) * LANES)
            y = _dot(lhs, rhs) + e["y_off"][:, pc] + e["xs_g"][:, pc] * dexp_ref[:, col]
            gated = y * _silu(z_ref[s, :, col].astype(F32))
            y_s[s, :, col] = gated
            ssq[s] = ssq[s] + gated * gated

    def stage4(s):
        inv = lax.rsqrt(jnp.sum(ssq[s], axis=-1, keepdims=True) / d_inner + RMS_EPS)
        out_ref[s] = (y_s[s] * inv).astype(BF16)

    for s0 in range(0, nb, group):
        seqs = range(s0, min(s0 + group, nb))
        insts = [(s, g) for s in seqs for g in range(G)]
        for s in seqs:
            stage0(s)
        for stage in (stage1, stage2, stage3):
            for s, g in insts:
                stage(s, g)
        for s in seqs:
            stage4(s)


def _ssd(proj, small, dtb_row, alog_row, dexp_row, expand, d_inner, col_idx):
    bsz, seq, _ = proj.shape
    L, nb = CHUNK, SSD_BATCH
    gn = SSD_GROUPS * SSD_STATE
    blk = lambda width, col: pl.BlockSpec((nb, L, width), lambda b, c: (b, c, col))
    full = lambda shape: pl.BlockSpec(shape, lambda b, c: (0,) * len(shape))
    return pl.pallas_call(
        functools.partial(_ssd_kernel, group=MIX_GROUP),
        grid=(bsz // nb, seq // L),
        in_specs=[blk(d_inner, col_idx["z"]), blk(d_inner, col_idx["xs"]),
                  blk(gn, col_idx["bm"]), blk(gn, col_idx["cm"]), blk(LANES, 0),
                  full((1, LANES)), full((1, LANES)),
                  full((1, d_inner)),
                  full((2 * LANES, d_inner))],
        out_specs=blk(d_inner, 0),
        out_shape=jax.ShapeDtypeStruct((bsz, seq, d_inner), BF16),
        scratch_shapes=[pltpu.VMEM((nb, SSD_GROUPS, SSD_STATE, d_inner // SSD_GROUPS), F32),
                        pltpu.VMEM((nb, L, d_inner), F32)],
        compiler_params=pltpu.CompilerParams(
            dimension_semantics=("parallel", "arbitrary"),
            vmem_limit_bytes=VMEM_LIMIT_BYTES, flags=MIXER_FLAGS),
        name="ssd",
    )(proj, proj, proj, proj, small, dtb_row, alog_row, dexp_row, expand)


def _mlstm_kernel(qk_ref, v_ref, o_ref, sm_ref, ifb_ref, nw_ref, out_ref, c_ref, n_ref, m_ref,
                  *, i_lane, f_lane):
    @pl.when(pl.program_id(1) == 0)
    def _():
        c_ref[...] = jnp.zeros(c_ref.shape, F32)
        n_ref[...] = jnp.zeros(n_ref.shape, F32)
        m_ref[...] = jnp.zeros(m_ref.shape, F32)

    nb = qk_ref.shape[0]
    L = CHUNK
    H = ML_HEADS
    d_qk = qk_ref.shape[2] // 2
    d_v = v_ref.shape[2]
    dk, dv = d_qk // H, d_v // H
    q_scale = dk ** -0.5
    assert 2.0 ** round(np.log2(q_scale)) == q_scale
    tril = _tril_mask(L)
    lane = lax.broadcasted_iota(jnp.int32, (L, LANES), 1)
    seq_gates, st = {}, {}

    def stage0(s):
        gates = sm_ref[s] + ifb_ref[...]
        b_cum = _cumsum_rows(-_softplus(-gates))
        mixed = jnp.where(lane < f_lane, gates, b_cum)
        seq_gates[s] = (mixed, mixed.T)

    def stage1(s, h):
        mixed, mixed_t = seq_gates[s]
        li_col = mixed[:, i_lane + h:i_lane + h + 1]
        li_row = mixed_t[i_lane + h:i_lane + h + 1, :]
        b_col = mixed[:, f_lane + h:f_lane + h + 1]
        b_row = mixed_t[f_lane + h:f_lane + h + 1, :]
        g_tot = b_col[L - 1:L, :]
        m_prev = m_ref[s, h, 0:1, 0:1]
        d = jnp.where(tril, (b_col - b_row) + li_row, -jnp.inf)
        m_inter = b_col + m_prev
        m_t = jnp.maximum(m_inter, jnp.max(d, axis=-1, keepdims=True))
        a_loc = (g_tot - b_col) + li_col
        m_loc = jnp.max(a_loc, axis=0, keepdims=True)
        m_new = jnp.maximum(g_tot + m_prev, m_loc)
        st[s, h] = dict(d=d, m_inter=m_inter, m_t=m_t, a_loc=a_loc, m_new=m_new,
                        sp=jnp.exp(g_tot + m_prev - m_new))

    def stage2(s, h):
        e = st[s, h]
        qb = qk_ref[s, :, h * dk:(h + 1) * dk] * q_scale
        kb = qk_ref[s, :, d_qk + h * dk:d_qk + (h + 1) * dk]
        e["scores"] = (lax.dot_general(qb, kb, NT_DIMS, preferred_element_type=F32)
                       * jnp.exp(e["d"] - e["m_t"]))
        e["w_inter"] = jnp.exp(e["m_inter"] - e["m_t"])

    def stage3(s, h):
        e = st[s, h]
        q = (qk_ref[s, :, h * dk:(h + 1) * dk] * q_scale).astype(F32)
        vb = v_ref[s, :, h * dv:(h + 1) * dv]
        lhs = jnp.concatenate([e["scores"].astype(BF16), (q * e["w_inter"]).astype(BF16)], axis=1)
        rhs = jnp.concatenate([vb, c_ref[s, h].astype(BF16)], axis=0)
        e["num"] = _dot(lhs, rhs)
        den = (jnp.sum(e["scores"], axis=-1, keepdims=True)
               + e["w_inter"] * jnp.sum(q * n_ref[s, h, 0:1, :], axis=-1, keepdims=True))
        e["den"] = jnp.maximum(jnp.abs(den), jnp.exp(-e["m_t"]))

    def stage4(s, h):
        e = st[s, h]
        k = qk_ref[s, :, d_qk + h * dk:d_qk + (h + 1) * dk].astype(F32)
        vb = v_ref[s, :, h * dv:(h + 1) * dv]
        kw = k * jnp.exp(e["a_loc"] - e["m_new"])
        c_ref[s, h] = e["sp"] * c_ref[s, h] + lax.dot_general(
            kw.astype(BF16), vb, TN_DIMS, preferred_element_type=F32)
        n_ref[s, h, 0:1, :] = e["sp"] * n_ref[s, h, 0:1, :] + jnp.sum(kw, axis=0, keepdims=True)
        m_ref[s, h] = jnp.broadcast_to(e["m_new"], m_ref.shape[2:])

    def stage5(s, h):
        e = st[s, h]
        vcols = slice(h * dv, (h + 1) * dv)
        inv_den = 1.0 / e["den"]
        msq = jnp.mean(e["num"] * e["num"], axis=-1, keepdims=True)
        row_scale = inv_den * lax.rsqrt(inv_den * inv_den * msq + RMS_EPS)
        gate = 0.5 * jnp.tanh(0.5 * o_ref[s, :, vcols]) + 0.5
        out_ref[s, :, vcols] = gate * (e["num"] * row_scale * nw_ref[:, vcols]).astype(BF16)

    insts = [(s, h) for s in range(nb) for h in range(H)]
    for s in range(nb):
        stage0(s)
    for stage in (stage1, stage2, stage3, stage4, stage5):
        for inst in insts:
            stage(*inst)


def _mlstm(proj, small, ifb_row, nw_row, d_qk, d_v, col_idx, i_lane, f_lane):
    bsz, seq, _ = proj.shape
    L, nb = CHUNK, MIX_BATCH
    H = ML_HEADS
    blk = lambda width, col: pl.BlockSpec((nb, L, width), lambda b, c: (b, c, col))
    full = lambda shape: pl.BlockSpec(shape, lambda b, c: (0,) * len(shape))
    return pl.pallas_call(
        functools.partial(_mlstm_kernel, i_lane=i_lane, f_lane=f_lane),
        grid=(bsz // nb, seq // L),
        in_specs=[blk(2 * d_qk, col_idx["qk"]), blk(d_v, col_idx["v"]), blk(d_v, col_idx["o"]),
                  blk(LANES, 0), full((1, LANES)), full((1, d_v))],
        out_specs=blk(d_v, 0),
        out_shape=jax.ShapeDtypeStruct((bsz, seq, d_v), BF16),
        scratch_shapes=[pltpu.VMEM((nb, H, d_qk // H, d_v // H), F32),
                        pltpu.VMEM((nb, H, SUBLANES, d_qk // H), F32),
                        pltpu.VMEM((nb, H, SUBLANES, LANES), F32)],
        compiler_params=pltpu.CompilerParams(
            dimension_semantics=("parallel", "arbitrary"),
            vmem_limit_bytes=VMEM_LIMIT_BYTES, flags=MIXER_FLAGS),
        name="mlstm",
    )(proj, proj, proj, small, ifb_row, nw_row)


def _merge_kernel(ys_ref, ym_ref, gs_ref, gm_ref, x_ref, mod_ref, wbs_ref, wbm_ref, wo_ref,
                  out_ref):
    a = _dot(ys_ref[...], wbs_ref[...])
    b = _dot(ym_ref[...], wbm_ref[...])
    merged = (_sigmoid(gs_ref[...].astype(F32)) * a + _sigmoid(gm_ref[...].astype(F32)) * b)
    o = _dot(merged.astype(BF16), wo_ref[...])
    out_ref[...] = x_ref[...] + mod_ref[0, 2:3, :] * o


def _merge(y_ssd, y_ml, proj, x2d, mod3, wbs, wbm, wo, seq, col_idx):
    t, d = x2d.shape
    dv = y_ssd.shape[1]
    tm = 512
    per_b = seq // tm
    const = lambda shape: pl.BlockSpec(shape, lambda i: (0,) * len(shape),
                                       pipeline_mode=pl.Buffered(1))
    return pl.pallas_call(
        _merge_kernel,
        grid=(t // tm,),
        in_specs=[pl.BlockSpec((tm, dv), lambda i: (i, 0)),
                  pl.BlockSpec((tm, dv), lambda i: (i, 0)),
                  pl.BlockSpec((tm, d), lambda i: (i, col_idx["g_ssd"])),
                  pl.BlockSpec((tm, d), lambda i: (i, col_idx["g_ml"])),
                  pl.BlockSpec((tm, d), lambda i: (i, 0)),
                  pl.BlockSpec((1, 6, d), lambda i: (i // per_b, 0, 0)),
                  const((dv, d)), const((dv, d)), const((d, d))],
        out_specs=pl.BlockSpec((tm, d), lambda i: (i, 0)),
        out_shape=jax.ShapeDtypeStruct((t, d), F32),
        compiler_params=pltpu.CompilerParams(
            dimension_semantics=("parallel",),
            vmem_limit_bytes=VMEM_LIMIT_BYTES),
        name="merge",
    )(y_ssd, y_ml, proj, proj, x2d, mod3, wbs, wbm, wo)


def _ffn_kernel(x_ref, mod_ref, wi_ref, wo_ref, fw_ref, out_ref, h_s, acc_s, *, ff_chunk):
    x1 = x_ref[...]
    shift = mod_ref[0, 3:4, :]
    scale = 1.0 + mod_ref[0, 4:5, :]
    gate = mod_ref[0, 5:6, :]
    h_s[...] = ((x1 * _rms_scale(x1)) * scale + shift).astype(BF16)
    d_ff = wo_ref.shape[0]
    for c in range(d_ff // ff_chunk):
        cc = slice(c * ff_chunk, (c + 1) * ff_chunk)
        gt = _dot(h_s[...], wi_ref[:, cc])
        up = _dot(h_s[...], wi_ref[:, d_ff + c * ff_chunk:d_ff + (c + 1) * ff_chunk])
        act = (_silu(gt) * up).astype(BF16)
        part = _dot(act, wo_ref[cc, :])
        if c == 0:
            acc_s[...] = part
        else:
            acc_s[...] += part
    x2 = x1 + gate * acc_s[...]
    out_ref[...] = (x2 * _rms_scale(x2)) * fw_ref[...]


def _ffn(x1, mod3, wi, wo, fw_row, seq, ff_chunk):
    t, d = x1.shape
    tm = 1024
    per_b = seq // tm
    const = lambda shape: pl.BlockSpec(shape, lambda i: (0,) * len(shape),
                                       pipeline_mode=pl.Buffered(1))
    return pl.pallas_call(
        functools.partial(_ffn_kernel, ff_chunk=ff_chunk),
        grid=(t // tm,),
        in_specs=[pl.BlockSpec((tm, d), lambda i: (i, 0)),
                  pl.BlockSpec((1, 6, d), lambda i: (i // per_b, 0, 0)),
                  const(wi.shape), const(wo.shape), const((1, d))],
        out_specs=pl.BlockSpec((tm, d), lambda i: (i, 0)),
        out_shape=jax.ShapeDtypeStruct((t, d), F32),
        scratch_shapes=[pltpu.VMEM((tm, d), BF16), pltpu.VMEM((tm, d), F32)],
        compiler_params=pltpu.CompilerParams(
            dimension_semantics=("parallel",),
            vmem_limit_bytes=VMEM_LIMIT_BYTES),
        name="ffn",
    )(x1, mod3, wi, wo, fw_row)


def _pad_lanes(row, offset=0):
    return jnp.zeros((1, LANES), F32).at[0, offset:offset + row.shape[0]].set(row)


def kernel(x, c, w_ada, b_ada, w_in, ssd_conv_w, ssd_conv_b, ssd_dt_bias, ssd_a_log, ssd_d,
           ssd_norm_w, mlstm_conv_w, mlstm_conv_b, mlstm_if_bias, mlstm_norm_w,
           w_branch_ssd, w_branch_mlstm, w_out, w_ffn_in, w_ffn_out, final_norm_w):
    bsz, seq, d = x.shape
    depth = w_ada.shape[0]
    d_inner = ssd_norm_w.shape[-1]
    n_heads = ssd_dt_bias.shape[-1]
    gn = SSD_GROUPS * SSD_STATE
    d_qk = mlstm_conv_w.shape[-1] // 2
    d_v = mlstm_norm_w.shape[-1]
    d_ff = w_ffn_out.shape[1]
    assert ssd_conv_w.shape[-1] == d_inner + 2 * gn and d_inner == n_heads * SSD_HEADDIM
    assert n_heads + 2 * ML_HEADS <= LANES and seq % CHUNK == 0 and bsz % MIX_BATCH == 0
    assert d_inner == 2 * d and d_v == 2 * d and 2 * d_qk == 2 * d and gn * 2 == d

    sizes = (d_inner, d_inner + 2 * gn, n_heads, 2 * d_qk, d_v, d_v, 2 * ML_HEADS, 2 * d)
    offs = np.concatenate([[0], np.cumsum(sizes)])
    assert w_in.shape[-1] == offs[-1]
    col_idx = {"z": 0, "xs": 1, "qk": 2, "v": 3, "o": 4,
               "g_ssd": 10, "g_ml": 11, "bm": 24, "cm": 25}
    i_lane, f_lane = n_heads, n_heads + ML_HEADS
    ff_chunk = 256

    expand = (jnp.arange(2 * LANES)[:, None] % LANES
              == (jnp.arange(d_inner)[None, :] // SSD_HEADDIM)).astype(BF16)

    x2d = x.reshape(bsz * seq, d)
    for l in range(depth):
        wl = w_in[l]
        tn = IN_PROJ_TN
        pieces = ((offs[0], offs[1] + d_inner),
                  (offs[3], offs[6]),
                  (offs[7], offs[8]),
                  (offs[1] + d_inner, offs[2]))
        assert all((b - a) % tn == 0 for a, b in pieces)
        w_main = _regroup_weights(wl, [int(c) for a, b in pieces for c in range(a, b, tn)], tn)
        w_small = jnp.concatenate(
            [wl[:, offs[2]:offs[3]], wl[:, offs[6]:offs[7]],
             jnp.zeros((d, LANES - n_heads - 2 * ML_HEADS), F32)], axis=1).astype(BF16)
        wi = w_ffn_in[l].astype(BF16)
        n_main = w_main.shape[1]
        conv_cols = ((d_inner, ssd_conv_w[l][:, :d_inner], ssd_conv_b[l][:d_inner]),
                     (2 * d_inner, mlstm_conv_w[l], mlstm_conv_b[l]),
                     (n_main - 2 * gn, ssd_conv_w[l][:, d_inner:], ssd_conv_b[l][d_inner:]))
        conv_w = jnp.zeros((CONV_K, n_main), F32)
        conv_b = jnp.zeros((1, n_main), F32)
        for off, cw, cb in conv_cols:
            conv_w = conv_w.at[:, off:off + cw.shape[1]].set(cw)
            conv_b = conv_b.at[0, off:off + cw.shape[1]].set(cb)
        conv_tiles = tuple(sorted({t for off, cw, _ in conv_cols
                                   for t in range(off // tn, (off + cw.shape[1] - 1) // tn + 1)}))

        mod3 = _adaln(c, w_ada[l], b_ada[l]).reshape(bsz, 6, d)
        proj, small = _in_proj(x2d, mod3, w_main, w_small, conv_w, conv_b, seq, conv_tiles)
        proj3 = proj.reshape(bsz, seq, n_main)
        small3 = small.reshape(bsz, seq, LANES)
        y_ssd = _ssd(proj3, small3, _pad_lanes(ssd_dt_bias[l]), _pad_lanes(ssd_a_log[l]),
                     jnp.repeat(ssd_d[l], SSD_HEADDIM).reshape(1, d_inner), expand, d_inner,
                     col_idx).reshape(bsz * seq, d_inner)
        y_ml = _mlstm(proj3, small3, _pad_lanes(mlstm_if_bias[l], i_lane),
                      mlstm_norm_w[l].reshape(1, d_v), d_qk, d_v, col_idx, i_lane, f_lane
                      ).reshape(bsz * seq, d_v)
        wbs = (ssd_norm_w[l][:, None] * w_branch_ssd[l]).astype(BF16)
        x1 = _merge(y_ssd, y_ml, proj, x2d, mod3, wbs, w_branch_mlstm[l].astype(BF16),
                    w_out[l].astype(BF16), seq, col_idx)
        assert l == depth - 1 == 0
        x2d = _ffn(x1, mod3, wi, w_ffn_out[l].astype(BF16), final_norm_w.reshape(1, d), seq,
                   ff_chunk)
    return x2d.reshape(bsz, seq, d)
```

```python
import functools

import numpy as np
import jax
import jax.numpy as jnp
from jax import lax
from jax.experimental import pallas as pl
from jax.experimental.pallas import tpu as pltpu

F32 = jnp.float32
BF16 = jnp.bfloat16
HIGHEST = lax.Precision.HIGHEST

LANES = 128
SUBLANES = 8
VMEM_LIMIT_BYTES = 56 * 2**20

RMS_EPS = 1e-6
LOG2E = 1.4426950408889634
CHUNK = 128
CONV_K = 4
HIST = SUBLANES
IN_PROJ_TN = 1024
MIX_BATCH = 4
SSD_BATCH = MIX_BATCH
MIX_GROUP = MIX_BATCH
MIXER_FLAGS = None
SSD_GROUPS = 4
SSD_STATE = 128
SSD_HEADDIM = 64
ML_HEADS = 4

NT_DIMS = (((1,), (1,)), ((), ()))
TN_DIMS = (((0,), (0,)), ((), ()))


def _dot(a, b):
    return jnp.dot(a, b, preferred_element_type=F32)


def _dot_nt(a, b):
    return lax.dot_general(a, b, NT_DIMS, preferred_element_type=F32)


def _sigmoid(x):
    return jax.nn.sigmoid(x)


def _silu(x):
    return x * _sigmoid(x)


def _softplus(x):
    return jnp.maximum(x, 0.0) + jnp.log1p(jnp.exp(-jnp.abs(x)))


def _rms_scale(x):
    return lax.rsqrt(jnp.mean(x * x, axis=-1, keepdims=True) + RMS_EPS)


def _split_bf16(v):
    hi = v.astype(BF16)
    lo = (v - hi.astype(F32)).astype(BF16)
    return hi, lo


def _tril_mask(n):
    row = lax.broadcasted_iota(jnp.int32, (n, n), 0)
    col = lax.broadcasted_iota(jnp.int32, (n, n), 1)
    return row >= col


def _cumsum_rows(x):
    n = x.shape[0]
    row = lax.broadcasted_iota(jnp.int32, x.shape, 0)
    k = 1
    while k < n:
        x = x + jnp.where(row >= k, pltpu.roll(x, k, 0), 0.0)
        k *= 2
    return x


def _adaln_kernel(c_ref, w_ref, b_ref, o_ref):
    c = c_ref[...]
    o_ref[...] = jnp.dot(_silu(c), w_ref[...], precision=HIGHEST,
                         preferred_element_type=F32) + b_ref[...]


def _adaln(c, w_ada, b_ada):
    bsz, d = c.shape
    n = w_ada.shape[1]
    tn = d
    return pl.pallas_call(
        _adaln_kernel,
        grid=(n // tn,),
        in_specs=[pl.BlockSpec((bsz, d), lambda j: (0, 0)),
                  pl.BlockSpec((d, tn), lambda j: (0, j)),
                  pl.BlockSpec((1, tn), lambda j: (0, j))],
        out_specs=pl.BlockSpec((bsz, tn), lambda j: (0, j)),
        out_shape=jax.ShapeDtypeStruct((bsz, n), F32),
        compiler_params=pltpu.CompilerParams(dimension_semantics=("parallel",)),
        name="adaln",
    )(c, w_ada, b_ada.reshape(1, n))


def _regroup_kernel(starts_ref, w_ref, o_ref):
    o_ref[...] = w_ref[...].astype(BF16)


def _regroup_rows(w_t, src_rows, tn):
    d = w_t.shape[1]
    assert all(r % SUBLANES == 0 for r in src_rows)
    starts = np.array([r // SUBLANES for r in src_rows], np.int32)
    grid_spec = pltpu.PrefetchScalarGridSpec(
        num_scalar_prefetch=1, grid=(len(src_rows),),
        in_specs=[pl.BlockSpec((pl.Element(tn), pl.Element(d)),
                               lambda j, st: (st[j] * SUBLANES, 0))],
        out_specs=pl.BlockSpec((tn, d), lambda j, st: (j, 0)))
    return pl.pallas_call(
        _regroup_kernel, grid_spec=grid_spec,
        out_shape=jax.ShapeDtypeStruct((tn * len(src_rows), d), BF16),
        compiler_params=pltpu.CompilerParams(dimension_semantics=("arbitrary",)),
        name="regroup")(jnp.asarray(starts), w_t)


def _conv_silu(buf_ref, cw_ref, cb_ref, t, r0, rows):
    cols = slice(t * LANES, (t + 1) * LANES)
    acc = cb_ref[:, cols]
    for j in range(CONV_K):
        s = HIST - (CONV_K - 1) + j + r0
        acc = acc + cw_ref[j:j + 1, cols] * buf_ref[t, s:s + rows, :]
    return _silu(acc)


def _in_proj_kernel(x_ref, mod_ref, w_ref, ws_ref, cw_ref, cb_ref, proj_ref, small_ref, h_ref,
                    buf_ref, buf2_ref, *, norm_rows, row_block, conv_block, conv_rows, conv_tiles):
    j = pl.program_id(1)
    tm = x_ref.shape[0]
    tn = w_ref.shape[0]

    assert 0 not in conv_tiles

    @pl.when(j == 0)
    def _():
        shift = mod_ref[0, 0:1, :]
        scale = 1.0 + mod_ref[0, 1:2, :]
        ws = ws_ref[...].astype(BF16)
        for rb in range(0, tm, row_block):
            for r0 in range(rb, rb + row_block, norm_rows):
                x = x_ref[r0:r0 + norm_rows, :]
                hb = ((x * _rms_scale(x)) * scale + shift).astype(BF16)
                h_ref[r0:r0 + norm_rows, :] = hb
                small_ref[r0:r0 + norm_rows, :] = _dot_nt(hb, ws)
            proj_ref[rb:rb + row_block, :] = _dot_nt(h_ref[rb:rb + row_block, :],
                                                     w_ref[...]).astype(BF16)

    is_conv = functools.reduce(jnp.logical_or, [j == t for t in conv_tiles])

    @pl.when(jnp.logical_not(is_conv) & (j != 0))
    def _():
        for rb in range(tm // row_block):
            rows = slice(rb * row_block, (rb + 1) * row_block)
            proj_ref[rows, :] = _dot_nt(h_ref[rows, :], w_ref[...]).astype(BF16)

    @pl.when(is_conv)
    def _():
        nblk = tn // LANES
        bufs = (buf_ref, buf2_ref)
        for t in range(nblk):
            buf_ref[t, 0:HIST, :] = jnp.zeros((HIST, LANES), F32)
        n_blocks = tm // conv_block

        def matmul(ib):
            return _dot_nt(h_ref[ib * conv_block:(ib + 1) * conv_block, :], w_ref[...])

        def stash(ib, res):
            cur, nxt = bufs[ib % 2], bufs[(ib + 1) % 2]
            for t in range(nblk):
                cur[t, HIST:HIST + conv_block, :] = res[:, t * LANES:(t + 1) * LANES]
            for t in range(nblk):
                nxt[t, 0:HIST, :] = res[conv_block - HIST:conv_block, t * LANES:(t + 1) * LANES]

        def epilogue(ib):
            rb = ib * conv_block
            for t in range(nblk):
                for r0 in range(0, conv_block, conv_rows):
                    proj_ref[rb + r0:rb + r0 + conv_rows, t * LANES:(t + 1) * LANES] = _conv_silu(
                        bufs[ib % 2], cw_ref, cb_ref, t, r0, conv_rows).astype(BF16)

        stash(0, matmul(0))
        for ib in range(1, n_blocks):
            res = matmul(ib)
            epilogue(ib - 1)
            stash(ib, res)
        epilogue(n_blocks - 1)


def _in_proj(x2d, mod3, w_main, w_small, conv_w, conv_b, seq, conv_tiles):
    t, d = x2d.shape
    n = w_main.shape[0]
    tm, tn = seq, IN_PROJ_TN
    row_block, conv_block, conv_rows = 512, 512, 256
    return pl.pallas_call(
        functools.partial(_in_proj_kernel, norm_rows=256, row_block=row_block,
                          conv_block=conv_block, conv_rows=conv_rows, conv_tiles=conv_tiles),
        grid=(t // tm, n // tn),
        in_specs=[pl.BlockSpec((tm, d), lambda i, j: (i, 0)),
                  pl.BlockSpec((1, 6, d), lambda i, j: (i, 0, 0)),
                  pl.BlockSpec((tn, d), lambda i, j: (j, 0)),
                  pl.BlockSpec((LANES, d), lambda i, j: (0, 0)),
                  pl.BlockSpec((CONV_K, tn), lambda i, j: (0, j)),
                  pl.BlockSpec((1, tn), lambda i, j: (0, j))],
        out_specs=[pl.BlockSpec((tm, tn), lambda i, j: (i, j)),
                   pl.BlockSpec((tm, LANES), lambda i, j: (i, 0))],
        out_shape=[jax.ShapeDtypeStruct((t, n), BF16),
                   jax.ShapeDtypeStruct((t, LANES), F32)],
        scratch_shapes=[pltpu.VMEM((tm, d), BF16),
                        pltpu.VMEM((tn // LANES, HIST + conv_block, LANES), F32),
                        pltpu.VMEM((tn // LANES, HIST + conv_block, LANES), F32)],
        compiler_params=pltpu.CompilerParams(
            dimension_semantics=("parallel", "arbitrary"),
            vmem_limit_bytes=VMEM_LIMIT_BYTES),
        name="in_proj",
    )(x2d, mod3, w_main, w_small, conv_w, conv_b)


def _ssd_kernel(z_ref, xs_ref, bm_ref, cm_ref, sm_ref, dtb_ref, alog_ref, dexp_ref,
                e_ref, out_ref, st_ref, y_s, *, group):
    @pl.when(pl.program_id(1) == 0)
    def _():
        st_ref[...] = jnp.zeros(st_ref.shape, F32)

    nb = z_ref.shape[0]
    L = CHUNK
    G, N, P = SSD_GROUPS, SSD_STATE, SSD_HEADDIM
    d_inner = xs_ref.shape[2]
    gw = d_inner // G
    n_heads = d_inner // P
    heads_per_tile = LANES // P
    tril = _tril_mask(L)
    tril_f = tril.astype(F32)
    lane = lax.broadcasted_iota(jnp.int32, (L, LANES), 1)
    lane_row = lax.broadcasted_iota(jnp.int32, (1, LANES), 1)
    head_mask = [((lane_row >= q * P) & (lane_row < (q + 1) * P)).astype(BF16)
                 for q in range(heads_per_tile)]
    sq, st = {}, {}
    ssq = {s: jnp.zeros((L, LANES), F32) for s in range(nb)}

    def stage0(s):
        dt = jnp.where(lane < n_heads, _softplus(sm_ref[s] + dtb_ref[...]), 0.0)
        a = dt * (-jnp.exp(alog_ref[...]))
        acs = jnp.dot(tril_f, a, precision=HIGHEST, preferred_element_type=F32)
        acs2 = acs * LOG2E
        a_tot = acs[L - 1:L, :]
        w_dt = jnp.exp(a_tot - acs) * dt
        e_tot = jnp.broadcast_to(jnp.exp(a_tot), (SUBLANES, LANES))
        stacked = jnp.concatenate([dt, jnp.exp2(acs2), w_dt, e_tot], axis=0)
        sq[s] = dict(acs2=acs2, acs2_t=acs2.T,
                     s_cat=jnp.concatenate(_split_bf16(stacked), axis=1))

    def stage1(s, g):
        gcols = slice(g * gw, (g + 1) * gw)
        ex = _dot(sq[s]["s_cat"], e_ref[:, gcols])
        xs_g = xs_ref[s, :, gcols].astype(F32)
        bm_g = bm_ref[s, :, g * N:(g + 1) * N]
        cm_g = cm_ref[s, :, g * N:(g + 1) * N]
        st[s, g] = dict(
            eacs_e=ex[L:2 * L], etot_e=ex[3 * L:3 * L + 1], xs_g=xs_g,
            xdt_g=(xs_g * ex[0:L]).astype(BF16), xw_g=(xs_g * ex[2 * L:3 * L]).astype(BF16),
            cb_g=lax.dot_general(cm_g, bm_g, NT_DIMS, preferred_element_type=F32))

    def stage2(s, g):
        e = st[s, g]
        bm_g = bm_ref[s, :, g * N:(g + 1) * N]
        cm_g = cm_ref[s, :, g * N:(g + 1) * N]
        st_g = st_ref[s, g]
        e["y_off"] = _dot(cm_g, st_g.astype(BF16)) * e["eacs_e"]
        st_ref[s, g] = st_g * e["etot_e"] + lax.dot_general(
            bm_g, e["xw_g"], TN_DIMS, preferred_element_type=F32)

    def stage3(s, g):
        e = st[s, g]
        acs2, acs2_t = sq[s]["acs2"], sq[s]["acs2_t"]
        for p in range(gw // LANES):
            pc = slice(p * LANES, (p + 1) * LANES)
            ms = []
            for q in range(heads_per_tile):
                h = (g * gw + p * LANES) // P + q
                seg = acs2[:, h:h + 1] - acs2_t[h:h + 1, :]
                dec = jnp.exp2(jnp.where(tril, seg, -jnp.inf))
                ms.append((e["cb_g"] * dec).astype(BF16))
            lhs = jnp.concatenate(ms, axis=1)
            xp = e["xdt_g"][:, pc]
            rhs = jnp.concatenate([xp * m for m in head_mask], axis=0)
            col = slice(g * gw + p * LANES, g * gw + (p + 1) * LANES)
            y = _dot(lhs, rhs) + e["y_off"][:, pc] + e["xs_g"][:, pc] * dexp_ref[:, col]
            gated = y * _silu(z_ref[s, :, col].astype(F32))
            y_s[s, :, col] = gated
            ssq[s] = ssq[s] + gated * gated

    def stage4(s):
        inv = lax.rsqrt(jnp.sum(ssq[s], axis=-1, keepdims=True) / d_inner + RMS_EPS)
        out_ref[s] = (y_s[s] * inv).astype(BF16)

    for s0 in range(0, nb, group):
        seqs = range(s0, min(s0 + group, nb))
        insts = [(s, g) for s in seqs for g in range(G)]
        for s in seqs:
            stage0(s)
        for stage in (stage1, stage2, stage3):
            for s, g in insts:
                stage(s, g)
        for s in seqs:
            stage4(s)


def _ssd(proj, small, dtb_row, alog_row, dexp_row, expand, d_inner, col_idx):
    bsz, seq, _ = proj.shape
    L, nb = CHUNK, SSD_BATCH
    gn = SSD_GROUPS * SSD_STATE
    blk = lambda width, col: pl.BlockSpec((nb, L, width), lambda b, c: (b, c, col))
    full = lambda shape: pl.BlockSpec(shape, lambda b, c: (0,) * len(shape))
    return pl.pallas_call(
        functools.partial(_ssd_kernel, group=MIX_GROUP),
        grid=(bsz // nb, seq // L),
        in_specs=[blk(d_inner, col_idx["z"]), blk(d_inner, col_idx["xs"]),
                  blk(gn, col_idx["bm"]), blk(gn, col_idx["cm"]), blk(LANES, 0),
                  full((1, LANES)), full((1, LANES)),
                  full((1, d_inner)),
                  full((2 * LANES, d_inner))],
        out_specs=blk(d_inner, 0),
        out_shape=jax.ShapeDtypeStruct((bsz, seq, d_inner), BF16),
        scratch_shapes=[pltpu.VMEM((nb, SSD_GROUPS, SSD_STATE, d_inner // SSD_GROUPS), F32),
                        pltpu.VMEM((nb, L, d_inner), F32)],
        compiler_params=pltpu.CompilerParams(
            dimension_semantics=("parallel", "arbitrary"),
            vmem_limit_bytes=VMEM_LIMIT_BYTES, flags=MIXER_FLAGS),
        name="ssd",
    )(proj, proj, proj, proj, small, dtb_row, alog_row, dexp_row, expand)


def _mlstm_kernel(qk_ref, v_ref, o_ref, sm_ref, ifb_ref, nw_ref, out_ref, c_ref, n_ref, m_ref,
                  *, i_lane, f_lane):
    @pl.when(pl.program_id(1) == 0)
    def _():
        c_ref[...] = jnp.zeros(c_ref.shape, F32)
        n_ref[...] = jnp.zeros(n_ref.shape, F32)
        m_ref[...] = jnp.zeros(m_ref.shape, F32)

    nb = qk_ref.shape[0]
    L = CHUNK
    H = ML_HEADS
    d_qk = qk_ref.shape[2] // 2
    d_v = v_ref.shape[2]
    dk, dv = d_qk // H, d_v // H
    q_scale = dk ** -0.5
    assert 2.0 ** round(np.log2(q_scale)) == q_scale
    tril = _tril_mask(L)
    lane = lax.broadcasted_iota(jnp.int32, (L, LANES), 1)
    seq_gates, st = {}, {}

    def stage0(s):
        gates = sm_ref[s] + ifb_ref[...]
        b_cum = _cumsum_rows(-_softplus(-gates))
        mixed = jnp.where(lane < f_lane, gates, b_cum)
        seq_gates[s] = (mixed, mixed.T)

    def stage1(s, h):
        mixed, mixed_t = seq_gates[s]
        li_col = mixed[:, i_lane + h:i_lane + h + 1]
        li_row = mixed_t[i_lane + h:i_lane + h + 1, :]
        b_col = mixed[:, f_lane + h:f_lane + h + 1]
        b_row = mixed_t[f_lane + h:f_lane + h + 1, :]
        g_tot = b_col[L - 1:L, :]
        m_prev = m_ref[s, h, 0:1, 0:1]
        d = jnp.where(tril, (b_col - b_row) + li_row, -jnp.inf)
        m_inter = b_col + m_prev
        m_t = jnp.maximum(m_inter, jnp.max(d, axis=-1, keepdims=True))
        a_loc = (g_tot - b_col) + li_col
        m_loc = jnp.max(a_loc, axis=0, keepdims=True)
        m_new = jnp.maximum(g_tot + m_prev, m_loc)
        st[s, h] = dict(d=d, m_inter=m_inter, m_t=m_t, a_loc=a_loc, m_new=m_new,
                        sp=jnp.exp(g_tot + m_prev - m_new))

    def stage2(s, h):
        e = st[s, h]
        qb = qk_ref[s, :, h * dk:(h + 1) * dk] * q_scale
        kb = qk_ref[s, :, d_qk + h * dk:d_qk + (h + 1) * dk]
        e["scores"] = (lax.dot_general(qb, kb, NT_DIMS, preferred_element_type=F32)
                       * jnp.exp(e["d"] - e["m_t"]))
        e["w_inter"] = jnp.exp(e["m_inter"] - e["m_t"])

    def stage3(s, h):
        e = st[s, h]
        q = (qk_ref[s, :, h * dk:(h + 1) * dk] * q_scale).astype(F32)
        vb = v_ref[s, :, h * dv:(h + 1) * dv]
        lhs = jnp.concatenate([e["scores"].astype(BF16), (q * e["w_inter"]).astype(BF16)], axis=1)
        rhs = jnp.concatenate([vb, c_ref[s, h].astype(BF16)], axis=0)
        e["num"] = _dot(lhs, rhs)
        den = (jnp.sum(e["scores"], axis=-1, keepdims=True)
               + e["w_inter"] * jnp.sum(q * n_ref[s, h, 0:1, :], axis=-1, keepdims=True))
        e["den"] = jnp.maximum(jnp.abs(den), jnp.exp(-e["m_t"]))

    def stage4(s, h):
        e = st[s, h]
        k = qk_ref[s, :, d_qk + h * dk:d_qk + (h + 1) * dk].astype(F32)
        vb = v_ref[s, :, h * dv:(h + 1) * dv]
        kw = k * jnp.exp(e["a_loc"] - e["m_new"])
        c_ref[s, h] = e["sp"] * c_ref[s, h] + lax.dot_general(
            kw.astype(BF16), vb, TN_DIMS, preferred_element_type=F32)
        n_ref[s, h, 0:1, :] = e["sp"] * n_ref[s, h, 0:1, :] + jnp.sum(kw, axis=0, keepdims=True)
        m_ref[s, h] = jnp.broadcast_to(e["m_new"], m_ref.shape[2:])

    def stage5(s, h):
        e = st[s, h]
        vcols = slice(h * dv, (h + 1) * dv)
        inv_den = 1.0 / e["den"]
        msq = jnp.mean(e["num"] * e["num"], axis=-1, keepdims=True)
        row_scale = inv_den * lax.rsqrt(inv_den * inv_den * msq + RMS_EPS)
        gate = 0.5 * jnp.tanh(0.5 * o_ref[s, :, vcols]) + 0.5
        out_ref[s, :, vcols] = gate * (e["num"] * row_scale * nw_ref[:, vcols]).astype(BF16)

    insts = [(s, h) for s in range(nb) for h in range(H)]
    for s in range(nb):
        stage0(s)
    for stage in (stage1, stage2, stage3, stage4, stage5):
        for inst in insts:
            stage(*inst)


def _mlstm(proj, small, ifb_row, nw_row, d_qk, d_v, col_idx, i_lane, f_lane):
    bsz, seq, _ = proj.shape
    L, nb = CHUNK, MIX_BATCH
    H = ML_HEADS
    blk = lambda width, col: pl.BlockSpec((nb, L, width), lambda b, c: (b, c, col))
    full = lambda shape: pl.BlockSpec(shape, lambda b, c: (0,) * len(shape))
    return pl.pallas_call(
        functools.partial(_mlstm_kernel, i_lane=i_lane, f_lane=f_lane),
        grid=(bsz // nb, seq // L),
        in_specs=[blk(2 * d_qk, col_idx["qk"]), blk(d_v, col_idx["v"]), blk(d_v, col_idx["o"]),
                  blk(LANES, 0), full((1, LANES)), full((1, d_v))],
        out_specs=blk(d_v, 0),
        out_shape=jax.ShapeDtypeStruct((bsz, seq, d_v), BF16),
        scratch_shapes=[pltpu.VMEM((nb, H, d_qk // H, d_v // H), F32),
                        pltpu.VMEM((nb, H, SUBLANES, d_qk // H), F32),
                        pltpu.VMEM((nb, H, SUBLANES, LANES), F32)],
        compiler_params=pltpu.CompilerParams(
            dimension_semantics=("parallel", "arbitrary"),
            vmem_limit_bytes=VMEM_LIMIT_BYTES, flags=MIXER_FLAGS),
        name="mlstm",
    )(proj, proj, proj, small, ifb_row, nw_row)


def _merge_kernel(ys_ref, ym_ref, gs_ref, gm_ref, x_ref, mod_ref, wbs_ref, wbm_ref, wo_ref,
                  out_ref):
    a = _dot(ys_ref[...], wbs_ref[...])
    b = _dot(ym_ref[...], wbm_ref[...])
    merged = (_sigmoid(gs_ref[...].astype(F32)) * a + _sigmoid(gm_ref[...].astype(F32)) * b)
    o = _dot(merged.astype(BF16), wo_ref[...])
    out_ref[...] = x_ref[...] + mod_ref[0, 2:3, :] * o


def _merge(y_ssd, y_ml, proj, x2d, mod3, wbs, wbm, wo, seq, col_idx):
    t, d = x2d.shape
    dv = y_ssd.shape[1]
    tm = 512
    per_b = seq // tm
    const = lambda shape: pl.BlockSpec(shape, lambda i: (0,) * len(shape),
                                       pipeline_mode=pl.Buffered(1))
    return pl.pallas_call(
        _merge_kernel,
        grid=(t // tm,),
        in_specs=[pl.BlockSpec((tm, dv), lambda i: (i, 0)),
                  pl.BlockSpec((tm, dv), lambda i: (i, 0)),
                  pl.BlockSpec((tm, d), lambda i: (i, col_idx["g_ssd"])),
                  pl.BlockSpec((tm, d), lambda i: (i, col_idx["g_ml"])),
                  pl.BlockSpec((tm, d), lambda i: (i, 0)),
                  pl.BlockSpec((1, 6, d), lambda i: (i // per_b, 0, 0)),
                  const((dv, d)), const((dv, d)), const((d, d))],
        out_specs=pl.BlockSpec((tm, d), lambda i: (i, 0)),
        out_shape=jax.ShapeDtypeStruct((t, d), F32),
        compiler_params=pltpu.CompilerParams(
            dimension_semantics=("parallel",),
            vmem_limit_bytes=VMEM_LIMIT_BYTES),
        name="merge",
    )(y_ssd, y_ml, proj, proj, x2d, mod3, wbs, wbm, wo)


def _ffn_kernel(x_ref, mod_ref, wi_ref, wo_ref, fw_ref, out_ref, h_s, acc_s, *, ff_chunk):
    x1 = x_ref[...]
    shift = mod_ref[0, 3:4, :]
    scale = 1.0 + mod_ref[0, 4:5, :]
    gate = mod_ref[0, 5:6, :]
    h_s[...] = ((x1 * _rms_scale(x1)) * scale + shift).astype(BF16)
    d_ff = wo_ref.shape[0]
    for c in range(d_ff // ff_chunk):
        cc = slice(c * ff_chunk, (c + 1) * ff_chunk)
        gt = _dot(h_s[...], wi_ref[:, cc])
        up = _dot(h_s[...], wi_ref[:, d_ff + c * ff_chunk:d_ff + (c + 1) * ff_chunk])
        act = (_silu(gt) * up).astype(BF16)
        part = _dot(act, wo_ref[cc, :])
        if c == 0:
            acc_s[...] = part
        else:
            acc_s[...] += part
    x2 = x1 + gate * acc_s[...]
    out_ref[...] = (x2 * _rms_scale(x2)) * fw_ref[...]


def _ffn(x1, mod3, wi, wo, fw_row, seq, ff_chunk):
    t, d = x1.shape
    tm = 1024
    per_b = seq // tm
    const = lambda shape: pl.BlockSpec(shape, lambda i: (0,) * len(shape),
                                       pipeline_mode=pl.Buffered(1))
    return pl.pallas_call(
        functools.partial(_ffn_kernel, ff_chunk=ff_chunk),
        grid=(t // tm,),
        in_specs=[pl.BlockSpec((tm, d), lambda i: (i, 0)),
                  pl.BlockSpec((1, 6, d), lambda i: (i // per_b, 0, 0)),
                  const(wi.shape), const(wo.shape), const((1, d))],
        out_specs=pl.BlockSpec((tm, d), lambda i: (i, 0)),
        out_shape=jax.ShapeDtypeStruct((t, d), F32),
        scratch_shapes=[pltpu.VMEM((tm, d), BF16), pltpu.VMEM((tm, d), F32)],
        compiler_params=pltpu.CompilerParams(
            dimension_semantics=("parallel",),
            vmem_limit_bytes=VMEM_LIMIT_BYTES),
        name="ffn",
    )(x1, mod3, wi, wo, fw_row)


def _pad_lanes(row, offset=0):
    return jnp.zeros((1, LANES), F32).at[0, offset:offset + row.shape[0]].set(row)


def kernel(x, c, w_ada, b_ada, w_in, ssd_conv_w, ssd_conv_b, ssd_dt_bias, ssd_a_log, ssd_d,
           ssd_norm_w, mlstm_conv_w, mlstm_conv_b, mlstm_if_bias, mlstm_norm_w,
           w_branch_ssd, w_branch_mlstm, w_out, w_ffn_in, w_ffn_out, final_norm_w):
    bsz, seq, d = x.shape
    depth = w_ada.shape[0]
    d_inner = ssd_norm_w.shape[-1]
    n_heads = ssd_dt_bias.shape[-1]
    gn = SSD_GROUPS * SSD_STATE
    d_qk = mlstm_conv_w.shape[-1] // 2
    d_v = mlstm_norm_w.shape[-1]
    d_ff = w_ffn_out.shape[1]
    assert ssd_conv_w.shape[-1] == d_inner + 2 * gn and d_inner == n_heads * SSD_HEADDIM
    assert n_heads + 2 * ML_HEADS <= LANES and seq % CHUNK == 0 and bsz % MIX_BATCH == 0
    assert d_inner == 2 * d and d_v == 2 * d and 2 * d_qk == 2 * d and gn * 2 == d

    sizes = (d_inner, d_inner + 2 * gn, n_heads, 2 * d_qk, d_v, d_v, 2 * ML_HEADS, 2 * d)
    offs = np.concatenate([[0], np.cumsum(sizes)])
    assert w_in.shape[-1] == offs[-1]
    col_idx = {"z": 0, "xs": 1, "qk": 2, "v": 3, "o": 4,
               "g_ssd": 10, "g_ml": 11, "bm": 24, "cm": 25}
    i_lane, f_lane = n_heads, n_heads + ML_HEADS
    ff_chunk = 256

    expand = (jnp.arange(2 * LANES)[:, None] % LANES
              == (jnp.arange(d_inner)[None, :] // SSD_HEADDIM)).astype(BF16)

    x2d = x.reshape(bsz * seq, d)
    for l in range(depth):
        w_t = jnp.swapaxes(w_in[l], 0, 1)
        tn = IN_PROJ_TN
        pieces = ((offs[0], offs[1] + d_inner),
                  (offs[3], offs[6]),
                  (offs[7], offs[8]),
                  (offs[1] + d_inner, offs[2]))
        assert all((b - a) % tn == 0 for a, b in pieces)
        w_main = _regroup_rows(w_t, [int(c) for a, b in pieces for c in range(a, b, tn)], tn)
        w_small = jnp.concatenate(
            [w_t[offs[2]:offs[3]], w_t[offs[6]:offs[7]],
             jnp.zeros((LANES - n_heads - 2 * ML_HEADS, d), F32)], axis=0)
        wi = w_ffn_in[l].astype(BF16)
        n_main = w_main.shape[0]
        conv_cols = ((d_inner, ssd_conv_w[l][:, :d_inner], ssd_conv_b[l][:d_inner]),
                     (2 * d_inner, mlstm_conv_w[l], mlstm_conv_b[l]),
                     (n_main - 2 * gn, ssd_conv_w[l][:, d_inner:], ssd_conv_b[l][d_inner:]))
        conv_w = jnp.zeros((CONV_K, n_main), F32)
        conv_b = jnp.zeros((1, n_main), F32)
        for off, cw, cb in conv_cols:
            conv_w = conv_w.at[:, off:off + cw.shape[1]].set(cw)
            conv_b = conv_b.at[0, off:off + cw.shape[1]].set(cb)
        conv_tiles = tuple(sorted({t for off, cw, _ in conv_cols
                                   for t in range(off // tn, (off + cw.shape[1] - 1) // tn + 1)}))

        mod3 = _adaln(c, w_ada[l], b_ada[l]).reshape(bsz, 6, d)
        proj, small = _in_proj(x2d, mod3, w_main, w_small, conv_w, conv_b, seq, conv_tiles)
        proj3 = proj.reshape(bsz, seq, n_main)
        small3 = small.reshape(bsz, seq, LANES)
        y_ssd = _ssd(proj3, small3, _pad_lanes(ssd_dt_bias[l]), _pad_lanes(ssd_a_log[l]),
                     jnp.repeat(ssd_d[l], SSD_HEADDIM).reshape(1, d_inner), expand, d_inner,
                     col_idx).reshape(bsz * seq, d_inner)
        y_ml = _mlstm(proj3, small3, _pad_lanes(mlstm_if_bias[l], i_lane),
                      mlstm_norm_w[l].reshape(1, d_v), d_qk, d_v, col_idx, i_lane, f_lane
                      ).reshape(bsz * seq, d_v)
        wbs = (ssd_norm_w[l][:, None] * w_branch_ssd[l]).astype(BF16)
        x1 = _merge(y_ssd, y_ml, proj, x2d, mod3, wbs, w_branch_mlstm[l].astype(BF16),
                    w_out[l].astype(BF16), seq, col_idx)
        assert l == depth - 1 == 0
        x2d = _ffn(x1, mod3, wi, w_ffn_out[l].astype(BF16), final_norm_w.reshape(1, d), seq,
                   ff_chunk)
    return x2d.reshape(bsz, seq, d)
```

```python
import functools

import numpy as np
import jax
import jax.numpy as jnp
from jax import lax
from jax.experimental import pallas as pl
from jax.experimental.pallas import tpu as pltpu

F32 = jnp.float32
BF16 = jnp.bfloat16
HIGHEST = lax.Precision.HIGHEST

LANES = 128
SUBLANES = 8
VMEM_LIMIT_BYTES = 56 * 2**20

RMS_EPS = 1e-6
LOG2E = 1.4426950408889634
CHUNK = 128
CONV_K = 4
HIST = SUBLANES
IN_PROJ_TN = 1024
MIX_BATCH = 4
SSD_BATCH = MIX_BATCH
SSD_TILE = 128
MIX_GROUP = MIX_BATCH
MIXER_FLAGS = None
SSD_GROUPS = 4
SSD_STATE = 128
SSD_HEADDIM = 64
ML_HEADS = 4

NT_DIMS = (((1,), (1,)), ((), ()))
TN_DIMS = (((0,), (0,)), ((), ()))


def _dot(a, b):
    return jnp.dot(a, b, preferred_element_type=F32)


def _dot_nt(a, b):
    return lax.dot_general(a, b, NT_DIMS, preferred_element_type=F32)


def _sigmoid(x):
    return jax.nn.sigmoid(x)


def _silu(x):
    return x * _sigmoid(x)


def _softplus(x):
    return jnp.maximum(x, 0.0) + jnp.log1p(jnp.exp(-jnp.abs(x)))


def _rms_scale(x):
    return lax.rsqrt(jnp.mean(x * x, axis=-1, keepdims=True) + RMS_EPS)


def _split_bf16(v):
    hi = v.astype(BF16)
    lo = (v - hi.astype(F32)).astype(BF16)
    return hi, lo


def _tril_mask(n):
    row = lax.broadcasted_iota(jnp.int32, (n, n), 0)
    col = lax.broadcasted_iota(jnp.int32, (n, n), 1)
    return row >= col


def _cumsum_rows(x):
    n = x.shape[0]
    row = lax.broadcasted_iota(jnp.int32, x.shape, 0)
    k = 1
    while k < n:
        x = x + jnp.where(row >= k, pltpu.roll(x, k, 0), 0.0)
        k *= 2
    return x


def _adaln_kernel(c_ref, w_ref, b_ref, o_ref):
    c = c_ref[...]
    o_ref[...] = jnp.dot(_silu(c), w_ref[...], precision=HIGHEST,
                         preferred_element_type=F32) + b_ref[...]


def _adaln(c, w_ada, b_ada):
    bsz, d = c.shape
    n = w_ada.shape[1]
    tn = d
    return pl.pallas_call(
        _adaln_kernel,
        grid=(n // tn,),
        in_specs=[pl.BlockSpec((bsz, d), lambda j: (0, 0)),
                  pl.BlockSpec((d, tn), lambda j: (0, j)),
                  pl.BlockSpec((1, tn), lambda j: (0, j))],
        out_specs=pl.BlockSpec((bsz, tn), lambda j: (0, j)),
        out_shape=jax.ShapeDtypeStruct((bsz, n), F32),
        compiler_params=pltpu.CompilerParams(dimension_semantics=("parallel",)),
        name="adaln",
    )(c, w_ada, b_ada.reshape(1, n))


def _regroup_kernel(starts_ref, w_ref, o_ref):
    o_ref[...] = w_ref[...].T.astype(BF16)


def _regroup_rows(w_t, src_rows, tn):
    d = w_t.shape[1]
    assert all(r % SUBLANES == 0 for r in src_rows)
    starts = np.array([r // SUBLANES for r in src_rows], np.int32)
    grid_spec = pltpu.PrefetchScalarGridSpec(
        num_scalar_prefetch=1, grid=(len(src_rows),),
        in_specs=[pl.BlockSpec((pl.Element(tn), pl.Element(d)),
                               lambda j, st: (st[j] * SUBLANES, 0))],
        out_specs=pl.BlockSpec((d, tn), lambda j, st: (0, j)))
    return pl.pallas_call(
        _regroup_kernel, grid_spec=grid_spec,
        out_shape=jax.ShapeDtypeStruct((d, tn * len(src_rows)), BF16),
        compiler_params=pltpu.CompilerParams(dimension_semantics=("arbitrary",)),
        name="regroup")(jnp.asarray(starts), w_t)


def _conv_silu(buf_ref, cw_ref, cb_ref, t, r0, rows):
    cols = slice(t * LANES, (t + 1) * LANES)
    acc = cb_ref[:, cols]
    for j in range(CONV_K):
        s = HIST - (CONV_K - 1) + j + r0
        acc = acc + cw_ref[j:j + 1, cols] * buf_ref[t, s:s + rows, :]
    return _silu(acc)


def _in_proj_kernel(x_ref, mod_ref, w_ref, ws_ref, cw_ref, cb_ref, proj_ref, small_ref, h_ref,
                    buf_ref, buf2_ref, *, norm_rows, row_block, conv_block, conv_rows, conv_tiles):
    j = pl.program_id(1)
    tm = x_ref.shape[0]
    tn = w_ref.shape[1]

    assert 0 not in conv_tiles

    @pl.when(j == 0)
    def _():
        shift = mod_ref[0, 0:1, :]
        scale = 1.0 + mod_ref[0, 1:2, :]
        ws = ws_ref[...].astype(BF16)
        for rb in range(0, tm, row_block):
            for r0 in range(rb, rb + row_block, norm_rows):
                x = x_ref[r0:r0 + norm_rows, :]
                hb = ((x * _rms_scale(x)) * scale + shift).astype(BF16)
                h_ref[r0:r0 + norm_rows, :] = hb
                small_ref[r0:r0 + norm_rows, :] = _dot(hb, ws)
            proj_ref[rb:rb + row_block, :] = _dot(h_ref[rb:rb + row_block, :],
                                                  w_ref[...]).astype(BF16)

    is_conv = functools.reduce(jnp.logical_or, [j == t for t in conv_tiles])

    @pl.when(jnp.logical_not(is_conv) & (j != 0))
    def _():
        for rb in range(tm // row_block):
            rows = slice(rb * row_block, (rb + 1) * row_block)
            proj_ref[rows, :] = _dot(h_ref[rows, :], w_ref[...]).astype(BF16)

    @pl.when(is_conv)
    def _():
        nblk = tn // LANES
        bufs = (buf_ref, buf2_ref)
        for t in range(nblk):
            buf_ref[t, 0:HIST, :] = jnp.zeros((HIST, LANES), F32)
        n_blocks = tm // conv_block

        def matmul(ib):
            return _dot(h_ref[ib * conv_block:(ib + 1) * conv_block, :], w_ref[...])

        def stash(ib, res):
            cur, nxt = bufs[ib % 2], bufs[(ib + 1) % 2]
            for t in range(nblk):
                cur[t, HIST:HIST + conv_block, :] = res[:, t * LANES:(t + 1) * LANES]
            for t in range(nblk):
                nxt[t, 0:HIST, :] = res[conv_block - HIST:conv_block, t * LANES:(t + 1) * LANES]

        def epilogue(ib):
            rb = ib * conv_block
            for t in range(nblk):
                for r0 in range(0, conv_block, conv_rows):
                    proj_ref[rb + r0:rb + r0 + conv_rows, t * LANES:(t + 1) * LANES] = _conv_silu(
                        bufs[ib % 2], cw_ref, cb_ref, t, r0, conv_rows).astype(BF16)

        stash(0, matmul(0))
        for ib in range(1, n_blocks):
            res = matmul(ib)
            epilogue(ib - 1)
            stash(ib, res)
        epilogue(n_blocks - 1)


def _in_proj(x2d, mod3, w_main, w_small, conv_w, conv_b, seq, conv_tiles):
    t, d = x2d.shape
    n = w_main.shape[1]
    tm, tn = seq, IN_PROJ_TN
    row_block, conv_block, conv_rows = 512, 512, 256
    return pl.pallas_call(
        functools.partial(_in_proj_kernel, norm_rows=256, row_block=row_block,
                          conv_block=conv_block, conv_rows=conv_rows, conv_tiles=conv_tiles),
        grid=(t // tm, n // tn),
        in_specs=[pl.BlockSpec((tm, d), lambda i, j: (i, 0)),
                  pl.BlockSpec((1, 6, d), lambda i, j: (i, 0, 0)),
                  pl.BlockSpec((d, tn), lambda i, j: (0, j)),
                  pl.BlockSpec((d, LANES), lambda i, j: (0, 0)),
                  pl.BlockSpec((CONV_K, tn), lambda i, j: (0, j)),
                  pl.BlockSpec((1, tn), lambda i, j: (0, j))],
        out_specs=[pl.BlockSpec((tm, tn), lambda i, j: (i, j)),
                   pl.BlockSpec((tm, LANES), lambda i, j: (i, 0))],
        out_shape=[jax.ShapeDtypeStruct((t, n), BF16),
                   jax.ShapeDtypeStruct((t, LANES), F32)],
        scratch_shapes=[pltpu.VMEM((tm, d), BF16),
                        pltpu.VMEM((tn // LANES, HIST + conv_block, LANES), F32),
                        pltpu.VMEM((tn // LANES, HIST + conv_block, LANES), F32)],
        compiler_params=pltpu.CompilerParams(
            dimension_semantics=("parallel", "arbitrary"),
            vmem_limit_bytes=VMEM_LIMIT_BYTES),
        name="in_proj",
    )(x2d, mod3, w_main, w_small, conv_w, conv_b)


def _ssd_kernel(z_ref, xs_ref, bm_ref, cm_ref, sm_ref, dtb_ref, alog_ref, dexp_ref,
                e_ref, out_ref, st_ref, y_s, *, group):
    @pl.when(pl.program_id(1) == 0)
    def _():
        st_ref[...] = jnp.zeros(st_ref.shape, F32)

    nb = z_ref.shape[0]
    L = CHUNK
    G, N, P = SSD_GROUPS, SSD_STATE, SSD_HEADDIM
    d_inner = xs_ref.shape[2]
    gw = d_inner // G
    n_heads = d_inner // P
    tile = SSD_TILE
    heads_per_tile = tile // P
    tril = _tril_mask(L)
    tril_f = tril.astype(F32)
    lane = lax.broadcasted_iota(jnp.int32, (L, LANES), 1)
    lane_row = lax.broadcasted_iota(jnp.int32, (1, tile), 1)
    head_mask = [((lane_row >= q * P) & (lane_row < (q + 1) * P)).astype(BF16)
                 for q in range(heads_per_tile)]
    sq, st = {}, {}
    ssq = {s: jnp.zeros((L, tile), F32) for s in range(nb)}

    def stage0(s):
        dt = jnp.where(lane < n_heads, _softplus(sm_ref[s] + dtb_ref[...]), 0.0)
        a = dt * (-jnp.exp(alog_ref[...]))
        acs = jnp.dot(tril_f, a, precision=HIGHEST, preferred_element_type=F32)
        acs2 = acs * LOG2E
        a_tot = acs[L - 1:L, :]
        w_dt = jnp.exp(a_tot - acs) * dt
        e_tot = jnp.broadcast_to(jnp.exp(a_tot), (SUBLANES, LANES))
        stacked = jnp.concatenate([dt, jnp.exp2(acs2), w_dt, e_tot], axis=0)
        sq[s] = dict(acs2=acs2, acs2_t=acs2.T,
                     s_cat=jnp.concatenate(_split_bf16(stacked), axis=1))

    def stage1(s, g):
        gcols = slice(g * gw, (g + 1) * gw)
        ex = _dot(sq[s]["s_cat"], e_ref[:, gcols])
        xs_g = xs_ref[s, :, gcols].astype(F32)
        bm_g = bm_ref[s, :, g * N:(g + 1) * N]
        cm_g = cm_ref[s, :, g * N:(g + 1) * N]
        st[s, g] = dict(
            eacs_e=ex[L:2 * L], etot_e=ex[3 * L:3 * L + 1], xs_g=xs_g,
            xdt_g=(xs_g * ex[0:L]).astype(BF16), xw_g=(xs_g * ex[2 * L:3 * L]).astype(BF16),
            cb_g=lax.dot_general(cm_g, bm_g, NT_DIMS, preferred_element_type=F32))

    def stage2(s, g):
        e = st[s, g]
        bm_g = bm_ref[s, :, g * N:(g + 1) * N]
        cm_g = cm_ref[s, :, g * N:(g + 1) * N]
        st_g = st_ref[s, g]
        e["y_off"] = _dot(cm_g, st_g.astype(BF16)) * e["eacs_e"]
        st_ref[s, g] = st_g * e["etot_e"] + lax.dot_general(
            bm_g, e["xw_g"], TN_DIMS, preferred_element_type=F32)

    def stage3(s, g):
        e = st[s, g]
        acs2, acs2_t = sq[s]["acs2"], sq[s]["acs2_t"]
        for p in range(gw // tile):
            pc = slice(p * tile, (p + 1) * tile)
            ms = []
            for q in range(heads_per_tile):
                h = (g * gw + p * tile) // P + q
                seg = acs2[:, h:h + 1] - acs2_t[h:h + 1, :]
                dec = jnp.exp2(jnp.where(tril, seg, -jnp.inf))
                ms.append((e["cb_g"] * dec).astype(BF16))
            lhs = jnp.concatenate(ms, axis=1)
            xp = e["xdt_g"][:, pc]
            rhs = jnp.concatenate([xp * m for m in head_mask], axis=0)
            col = slice(g * gw + p * tile, g * gw + (p + 1) * tile)
            y = _dot(lhs, rhs) + e["y_off"][:, pc] + e["xs_g"][:, pc] * dexp_ref[:, col]
            gated = y * _silu(z_ref[s, :, col].astype(F32))
            y_s[s, :, col] = gated
            ssq[s] = ssq[s] + gated * gated

    def stage4(s):
        inv = lax.rsqrt(jnp.sum(ssq[s], axis=-1, keepdims=True) / d_inner + RMS_EPS)
        out_ref[s] = (y_s[s] * inv).astype(BF16)

    for s0 in range(0, nb, group):
        seqs = range(s0, min(s0 + group, nb))
        insts = [(s, g) for s in seqs for g in range(G)]
        for s in seqs:
            stage0(s)
        for stage in (stage1, stage2, stage3):
            for s, g in insts:
                stage(s, g)
        for s in seqs:
            stage4(s)


def _ssd(proj, small, dtb_row, alog_row, dexp_row, expand, d_inner, col_idx):
    bsz, seq, _ = proj.shape
    L, nb = CHUNK, SSD_BATCH
    gn = SSD_GROUPS * SSD_STATE
    blk = lambda width, col: pl.BlockSpec((nb, L, width), lambda b, c: (b, c, col))
    full = lambda shape: pl.BlockSpec(shape, lambda b, c: (0,) * len(shape))
    return pl.pallas_call(
        functools.partial(_ssd_kernel, group=MIX_GROUP),
        grid=(bsz // nb, seq // L),
        in_specs=[blk(d_inner, col_idx["z"]), blk(d_inner, col_idx["xs"]),
                  blk(gn, col_idx["bm"]), blk(gn, col_idx["cm"]), blk(LANES, 0),
                  full((1, LANES)), full((1, LANES)),
                  full((1, d_inner)),
                  full((2 * LANES, d_inner))],
        out_specs=blk(d_inner, 0),
        out_shape=jax.ShapeDtypeStruct((bsz, seq, d_inner), BF16),
        scratch_shapes=[pltpu.VMEM((nb, SSD_GROUPS, SSD_STATE, d_inner // SSD_GROUPS), F32),
                        pltpu.VMEM((nb, L, d_inner), F32)],
        compiler_params=pltpu.CompilerParams(
            dimension_semantics=("parallel", "arbitrary"),
            vmem_limit_bytes=VMEM_LIMIT_BYTES, flags=MIXER_FLAGS),
        name="ssd",
    )(proj, proj, proj, proj, small, dtb_row, alog_row, dexp_row, expand)


def _mlstm_kernel(qk_ref, v_ref, o_ref, sm_ref, ifb_ref, nw_ref, out_ref, c_ref, n_ref, m_ref,
                  *, i_lane, f_lane):
    @pl.when(pl.program_id(1) == 0)
    def _():
        c_ref[...] = jnp.zeros(c_ref.shape, F32)
        n_ref[...] = jnp.zeros(n_ref.shape, F32)
        m_ref[...] = jnp.zeros(m_ref.shape, F32)

    nb = qk_ref.shape[0]
    L = CHUNK
    H = ML_HEADS
    d_qk = qk_ref.shape[2] // 2
    d_v = v_ref.shape[2]
    dk, dv = d_qk // H, d_v // H
    q_scale = dk ** -0.5
    assert 2.0 ** round(np.log2(q_scale)) == q_scale
    tril = _tril_mask(L)
    lane = lax.broadcasted_iota(jnp.int32, (L, LANES), 1)
    seq_gates, st = {}, {}

    def stage0(s):
        gates = sm_ref[s] + ifb_ref[...]
        b_cum = _cumsum_rows(-_softplus(-gates))
        mixed = jnp.where(lane < f_lane, gates, b_cum)
        seq_gates[s] = (mixed, mixed.T)

    def stage1(s, h):
        mixed, mixed_t = seq_gates[s]
        li_col = mixed[:, i_lane + h:i_lane + h + 1]
        li_row = mixed_t[i_lane + h:i_lane + h + 1, :]
        b_col = mixed[:, f_lane + h:f_lane + h + 1]
        b_row = mixed_t[f_lane + h:f_lane + h + 1, :]
        g_tot = b_col[L - 1:L, :]
        m_prev = m_ref[s, h, 0:1, 0:1]
        d = jnp.where(tril, (b_col - b_row) + li_row, -jnp.inf)
        m_inter = b_col + m_prev
        m_t = jnp.maximum(m_inter, jnp.max(d, axis=-1, keepdims=True))
        a_loc = (g_tot - b_col) + li_col
        m_loc = jnp.max(a_loc, axis=0, keepdims=True)
        m_new = jnp.maximum(g_tot + m_prev, m_loc)
        st[s, h] = dict(d=d, m_inter=m_inter, m_t=m_t, a_loc=a_loc, m_new=m_new,
                        sp=jnp.exp(g_tot + m_prev - m_new))

    def stage2(s, h):
        e = st[s, h]
        qb = qk_ref[s, :, h * dk:(h + 1) * dk] * q_scale
        kb = qk_ref[s, :, d_qk + h * dk:d_qk + (h + 1) * dk]
        e["scores"] = (lax.dot_general(qb, kb, NT_DIMS, preferred_element_type=F32)
                       * jnp.exp(e["d"] - e["m_t"]))
        e["w_inter"] = jnp.exp(e["m_inter"] - e["m_t"])

    def stage3(s, h):
        e = st[s, h]
        q = (qk_ref[s, :, h * dk:(h + 1) * dk] * q_scale).astype(F32)
        vb = v_ref[s, :, h * dv:(h + 1) * dv]
        lhs = jnp.concatenate([e["scores"].astype(BF16), (q * e["w_inter"]).astype(BF16)], axis=1)
        rhs = jnp.concatenate([vb, c_ref[s, h].astype(BF16)], axis=0)
        e["num"] = _dot(lhs, rhs)
        den = (jnp.sum(e["scores"], axis=-1, keepdims=True)
               + e["w_inter"] * jnp.sum(q * n_ref[s, h, 0:1, :], axis=-1, keepdims=True))
        e["den"] = jnp.maximum(jnp.abs(den), jnp.exp(-e["m_t"]))

    def stage4(s, h):
        e = st[s, h]
        k = qk_ref[s, :, d_qk + h * dk:d_qk + (h + 1) * dk].astype(F32)
        vb = v_ref[s, :, h * dv:(h + 1) * dv]
        kw = k * jnp.exp(e["a_loc"] - e["m_new"])
        c_ref[s, h] = e["sp"] * c_ref[s, h] + lax.dot_general(
            kw.astype(BF16), vb, TN_DIMS, preferred_element_type=F32)
        n_ref[s, h, 0:1, :] = e["sp"] * n_ref[s, h, 0:1, :] + jnp.sum(kw, axis=0, keepdims=True)
        m_ref[s, h] = jnp.broadcast_to(e["m_new"], m_ref.shape[2:])

    def stage5(s, h):
        e = st[s, h]
        vcols = slice(h * dv, (h + 1) * dv)
        inv_den = 1.0 / e["den"]
        msq = jnp.mean(e["num"] * e["num"], axis=-1, keepdims=True)
        row_scale = inv_den * lax.rsqrt(inv_den * inv_den * msq + RMS_EPS)
        gate = 0.5 * jnp.tanh(0.5 * o_ref[s, :, vcols]) + 0.5
        out_ref[s, :, vcols] = gate * (e["num"] * row_scale * nw_ref[:, vcols]).astype(BF16)

    insts = [(s, h) for s in range(nb) for h in range(H)]
    for s in range(nb):
        stage0(s)
    for stage in (stage1, stage2, stage3, stage4, stage5):
        for inst in insts:
            stage(*inst)


def _mlstm(proj, small, ifb_row, nw_row, d_qk, d_v, col_idx, i_lane, f_lane):
    bsz, seq, _ = proj.shape
    L, nb = CHUNK, MIX_BATCH
    H = ML_HEADS
    blk = lambda width, col: pl.BlockSpec((nb, L, width), lambda b, c: (b, c, col))
    full = lambda shape: pl.BlockSpec(shape, lambda b, c: (0,) * len(shape))
    return pl.pallas_call(
        functools.partial(_mlstm_kernel, i_lane=i_lane, f_lane=f_lane),
        grid=(bsz // nb, seq // L),
        in_specs=[blk(2 * d_qk, col_idx["qk"]), blk(d_v, col_idx["v"]), blk(d_v, col_idx["o"]),
                  blk(LANES, 0), full((1, LANES)), full((1, d_v))],
        out_specs=blk(d_v, 0),
        out_shape=jax.ShapeDtypeStruct((bsz, seq, d_v), BF16),
        scratch_shapes=[pltpu.VMEM((nb, H, d_qk // H, d_v // H), F32),
                        pltpu.VMEM((nb, H, SUBLANES, d_qk // H), F32),
                        pltpu.VMEM((nb, H, SUBLANES, LANES), F32)],
        compiler_params=pltpu.CompilerParams(
            dimension_semantics=("parallel", "arbitrary"),
            vmem_limit_bytes=VMEM_LIMIT_BYTES, flags=MIXER_FLAGS),
        name="mlstm",
    )(proj, proj, proj, small, ifb_row, nw_row)


def _merge_kernel(ys_ref, ym_ref, gs_ref, gm_ref, x_ref, mod_ref, wbs_ref, wbm_ref, wo_ref,
                  out_ref):
    a = _dot(ys_ref[...], wbs_ref[...])
    b = _dot(ym_ref[...], wbm_ref[...])
    merged = (_sigmoid(gs_ref[...].astype(F32)) * a + _sigmoid(gm_ref[...].astype(F32)) * b)
    o = _dot(merged.astype(BF16), wo_ref[...])
    out_ref[...] = x_ref[...] + mod_ref[0, 2:3, :] * o


def _merge(y_ssd, y_ml, proj, x2d, mod3, wbs, wbm, wo, seq, col_idx):
    t, d = x2d.shape
    dv = y_ssd.shape[1]
    tm = 512
    per_b = seq // tm
    const = lambda shape: pl.BlockSpec(shape, lambda i: (0,) * len(shape),
                                       pipeline_mode=pl.Buffered(1))
    return pl.pallas_call(
        _merge_kernel,
        grid=(t // tm,),
        in_specs=[pl.BlockSpec((tm, dv), lambda i: (i, 0)),
                  pl.BlockSpec((tm, dv), lambda i: (i, 0)),
                  pl.BlockSpec((tm, d), lambda i: (i, col_idx["g_ssd"])),
                  pl.BlockSpec((tm, d), lambda i: (i, col_idx["g_ml"])),
                  pl.BlockSpec((tm, d), lambda i: (i, 0)),
                  pl.BlockSpec((1, 6, d), lambda i: (i // per_b, 0, 0)),
                  const((dv, d)), const((dv, d)), const((d, d))],
        out_specs=pl.BlockSpec((tm, d), lambda i: (i, 0)),
        out_shape=jax.ShapeDtypeStruct((t, d), F32),
        compiler_params=pltpu.CompilerParams(
            dimension_semantics=("parallel",),
            vmem_limit_bytes=VMEM_LIMIT_BYTES),
        name="merge",
    )(y_ssd, y_ml, proj, proj, x2d, mod3, wbs, wbm, wo)


def _ffn_kernel(x_ref, mod_ref, wi_ref, wo_ref, fw_ref, out_ref, h_s, acc_s, *, ff_chunk):
    tm = x_ref.shape[0]
    shift = mod_ref[0, 3:4, :]
    scale = 1.0 + mod_ref[0, 4:5, :]
    gate = mod_ref[0, 5:6, :]
    d_ff = wo_ref.shape[0]
    n_chunks = d_ff // ff_chunk
    halves = (slice(0, tm // 2), slice(tm // 2, tm))

    def prologue(r):
        x1 = x_ref[r, :]
        h_s[r, :] = ((x1 * _rms_scale(x1)) * scale + shift).astype(BF16)

    def chunk(r, c):
        cc = slice(c * ff_chunk, (c + 1) * ff_chunk)
        gt = _dot(h_s[r, :], wi_ref[:, cc])
        up = _dot(h_s[r, :], wi_ref[:, d_ff + c * ff_chunk:d_ff + (c + 1) * ff_chunk])
        act = (_silu(gt) * up).astype(BF16)
        part = _dot(act, wo_ref[cc, :])
        if c == 0:
            acc_s[r, :] = part
        else:
            acc_s[r, :] += part

    def epilogue(r):
        x2 = x_ref[r, :] + gate * acc_s[r, :]
        out_ref[r, :] = (x2 * _rms_scale(x2)) * fw_ref[...]

    a, b = halves
    prologue(a)
    chunk(a, 0)
    prologue(b)
    for c in range(n_chunks):
        if c == n_chunks - 1:
            epilogue(a)
        chunk(b, c)
        if c + 1 < n_chunks:
            chunk(a, c + 1)
    epilogue(b)


def _ffn(x1, mod3, wi, wo, fw_row, seq, ff_chunk):
    t, d = x1.shape
    tm = 1024
    per_b = seq // tm
    const = lambda shape: pl.BlockSpec(shape, lambda i: (0,) * len(shape),
                                       pipeline_mode=pl.Buffered(1))
    return pl.pallas_call(
        functools.partial(_ffn_kernel, ff_chunk=ff_chunk),
        grid=(t // tm,),
        in_specs=[pl.BlockSpec((tm, d), lambda i: (i, 0)),
                  pl.BlockSpec((1, 6, d), lambda i: (i // per_b, 0, 0)),
                  const(wi.shape), const(wo.shape), const((1, d))],
        out_specs=pl.BlockSpec((tm, d), lambda i: (i, 0)),
        out_shape=jax.ShapeDtypeStruct((t, d), F32),
        scratch_shapes=[pltpu.VMEM((tm, d), BF16), pltpu.VMEM((tm, d), F32)],
        compiler_params=pltpu.CompilerParams(
            dimension_semantics=("parallel",),
            vmem_limit_bytes=VMEM_LIMIT_BYTES),
        name="ffn",
    )(x1, mod3, wi, wo, fw_row)


def _pad_lanes(row, offset=0):
    return jnp.zeros((1, LANES), F32).at[0, offset:offset + row.shape[0]].set(row)


def kernel(x, c, w_ada, b_ada, w_in, ssd_conv_w, ssd_conv_b, ssd_dt_bias, ssd_a_log, ssd_d,
           ssd_norm_w, mlstm_conv_w, mlstm_conv_b, mlstm_if_bias, mlstm_norm_w,
           w_branch_ssd, w_branch_mlstm, w_out, w_ffn_in, w_ffn_out, final_norm_w):
    bsz, seq, d = x.shape
    depth = w_ada.shape[0]
    d_inner = ssd_norm_w.shape[-1]
    n_heads = ssd_dt_bias.shape[-1]
    gn = SSD_GROUPS * SSD_STATE
    d_qk = mlstm_conv_w.shape[-1] // 2
    d_v = mlstm_norm_w.shape[-1]
    d_ff = w_ffn_out.shape[1]
    assert ssd_conv_w.shape[-1] == d_inner + 2 * gn and d_inner == n_heads * SSD_HEADDIM
    assert n_heads + 2 * ML_HEADS <= LANES and seq % CHUNK == 0 and bsz % MIX_BATCH == 0
    assert d_inner == 2 * d and d_v == 2 * d and 2 * d_qk == 2 * d and gn * 2 == d

    sizes = (d_inner, d_inner + 2 * gn, n_heads, 2 * d_qk, d_v, d_v, 2 * ML_HEADS, 2 * d)
    offs = np.concatenate([[0], np.cumsum(sizes)])
    assert w_in.shape[-1] == offs[-1]
    col_idx = {"z": 0, "xs": 1, "qk": 2, "v": 3, "o": 4,
               "g_ssd": 10, "g_ml": 11, "bm": 24, "cm": 25}
    i_lane, f_lane = n_heads, n_heads + ML_HEADS
    ff_chunk = 256

    expand = (jnp.arange(2 * LANES)[:, None] % LANES
              == (jnp.arange(d_inner)[None, :] // SSD_HEADDIM)).astype(BF16)

    x2d = x.reshape(bsz * seq, d)
    for l in range(depth):
        w_t = jnp.swapaxes(w_in[l], 0, 1)
        tn = IN_PROJ_TN
        pieces = ((offs[0], offs[1] + d_inner),
                  (offs[3], offs[6]),
                  (offs[7], offs[8]),
                  (offs[1] + d_inner, offs[2]))
        assert all((b - a) % tn == 0 for a, b in pieces)
        w_main = _regroup_rows(w_t, [int(c) for a, b in pieces for c in range(a, b, tn)], tn)
        w_small = jnp.concatenate(
            [w_t[offs[2]:offs[3]], w_t[offs[6]:offs[7]],
             jnp.zeros((LANES - n_heads - 2 * ML_HEADS, d), F32)], axis=0).T
        wi = w_ffn_in[l].astype(BF16)
        n_main = w_main.shape[1]
        conv_cols = ((d_inner, ssd_conv_w[l][:, :d_inner], ssd_conv_b[l][:d_inner]),
                     (2 * d_inner, mlstm_conv_w[l], mlstm_conv_b[l]),
                     (n_main - 2 * gn, ssd_conv_w[l][:, d_inner:], ssd_conv_b[l][d_inner:]))
        conv_w = jnp.zeros((CONV_K, n_main), F32)
        conv_b = jnp.zeros((1, n_main), F32)
        for off, cw, cb in conv_cols:
            conv_w = conv_w.at[:, off:off + cw.shape[1]].set(cw)
            conv_b = conv_b.at[0, off:off + cw.shape[1]].set(cb)
        conv_tiles = tuple(sorted({t for off, cw, _ in conv_cols
                                   for t in range(off // tn, (off + cw.shape[1] - 1) // tn + 1)}))

        mod3 = _adaln(c, w_ada[l], b_ada[l]).reshape(bsz, 6, d)
        proj, small = _in_proj(x2d, mod3, w_main, w_small, conv_w, conv_b, seq, conv_tiles)
        proj3 = proj.reshape(bsz, seq, n_main)
        small3 = small.reshape(bsz, seq, LANES)
        y_ssd = _ssd(proj3, small3, _pad_lanes(ssd_dt_bias[l]), _pad_lanes(ssd_a_log[l]),
                     jnp.repeat(ssd_d[l], SSD_HEADDIM).reshape(1, d_inner), expand, d_inner,
                     col_idx).reshape(bsz * seq, d_inner)
        y_ml = _mlstm(proj3, small3, _pad_lanes(mlstm_if_bias[l], i_lane),
                      mlstm_norm_w[l].reshape(1, d_v), d_qk, d_v, col_idx, i_lane, f_lane
                      ).reshape(bsz * seq, d_v)
        wbs = (ssd_norm_w[l][:, None] * w_branch_ssd[l]).astype(BF16)
        x1 = _merge(y_ssd, y_ml, proj, x2d, mod3, wbs, w_branch_mlstm[l].astype(BF16),
                    w_out[l].astype(BF16), seq, col_idx)
        assert l == depth - 1 == 0
        x2d = _ffn(x1, mod3, wi, w_ffn_out[l].astype(BF16), final_norm_w.reshape(1, d), seq,
                   ff_chunk)
    return x2d.reshape(bsz, seq, d)
```

```python
import functools

import numpy as np
import jax
import jax.numpy as jnp
from jax import lax
from jax.experimental import pallas as pl
from jax.experimental.pallas import tpu as pltpu

F32 = jnp.float32
BF16 = jnp.bfloat16
HIGHEST = lax.Precision.HIGHEST

LANES = 128
SUBLANES = 8
VMEM_LIMIT_BYTES = 56 * 2**20

RMS_EPS = 1e-6
LOG2E = 1.4426950408889634
CHUNK = 128
CONV_K = 4
HIST = SUBLANES
IN_PROJ_TN = 1024
MIX_BATCH = 4
SSD_BATCH = MIX_BATCH
SSD_TILE = 128
MIX_GROUP = MIX_BATCH
MIXER_FLAGS = None
SSD_GROUPS = 4
SSD_STATE = 128
SSD_HEADDIM = 64
ML_HEADS = 4

NT_DIMS = (((1,), (1,)), ((), ()))
TN_DIMS = (((0,), (0,)), ((), ()))


def _dot(a, b):
    return jnp.dot(a, b, preferred_element_type=F32)


def _dot_nt(a, b):
    return lax.dot_general(a, b, NT_DIMS, preferred_element_type=F32)


def _sigmoid(x):
    return jax.nn.sigmoid(x)


def _silu(x):
    return x * _sigmoid(x)


def _softplus(x):
    return jnp.maximum(x, 0.0) + jnp.log1p(jnp.exp(-jnp.abs(x)))


def _rms_scale(x):
    return lax.rsqrt(jnp.mean(x * x, axis=-1, keepdims=True) + RMS_EPS)


def _split_bf16(v):
    hi = v.astype(BF16)
    lo = (v - hi.astype(F32)).astype(BF16)
    return hi, lo


def _tril_mask(n):
    row = lax.broadcasted_iota(jnp.int32, (n, n), 0)
    col = lax.broadcasted_iota(jnp.int32, (n, n), 1)
    return row >= col


def _cumsum_rows(x):
    n = x.shape[0]
    row = lax.broadcasted_iota(jnp.int32, x.shape, 0)
    k = 1
    while k < n:
        x = x + jnp.where(row >= k, pltpu.roll(x, k, 0), 0.0)
        k *= 2
    return x


def _adaln_kernel(c_ref, w_ref, b_ref, o_ref):
    c = c_ref[...]
    o_ref[...] = jnp.dot(_silu(c), w_ref[...], precision=HIGHEST,
                         preferred_element_type=F32) + b_ref[...]


def _adaln(c, w_ada, b_ada):
    bsz, d = c.shape
    n = w_ada.shape[1]
    tn = d
    return pl.pallas_call(
        _adaln_kernel,
        grid=(n // tn,),
        in_specs=[pl.BlockSpec((bsz, d), lambda j: (0, 0)),
                  pl.BlockSpec((d, tn), lambda j: (0, j)),
                  pl.BlockSpec((1, tn), lambda j: (0, j))],
        out_specs=pl.BlockSpec((bsz, tn), lambda j: (0, j)),
        out_shape=jax.ShapeDtypeStruct((bsz, n), F32),
        compiler_params=pltpu.CompilerParams(dimension_semantics=("parallel",)),
        name="adaln",
    )(c, w_ada, b_ada.reshape(1, n))


def _regroup_kernel(starts_ref, w_ref, o_ref):
    o_ref[...] = w_ref[...].T.astype(BF16)


def _regroup_rows(w_t, src_rows, tn):
    d = w_t.shape[1]
    assert all(r % SUBLANES == 0 for r in src_rows)
    starts = np.array([r // SUBLANES for r in src_rows], np.int32)
    grid_spec = pltpu.PrefetchScalarGridSpec(
        num_scalar_prefetch=1, grid=(len(src_rows),),
        in_specs=[pl.BlockSpec((pl.Element(tn), pl.Element(d)),
                               lambda j, st: (st[j] * SUBLANES, 0))],
        out_specs=pl.BlockSpec((d, tn), lambda j, st: (0, j)))
    return pl.pallas_call(
        _regroup_kernel, grid_spec=grid_spec,
        out_shape=jax.ShapeDtypeStruct((d, tn * len(src_rows)), BF16),
        compiler_params=pltpu.CompilerParams(dimension_semantics=("arbitrary",)),
        name="regroup")(jnp.asarray(starts), w_t)


def _conv_silu(buf_ref, cw_ref, cb_ref, t, r0, rows):
    cols = slice(t * LANES, (t + 1) * LANES)
    acc = cb_ref[:, cols]
    for j in range(CONV_K):
        s = HIST - (CONV_K - 1) + j + r0
        acc = acc + cw_ref[j:j + 1, cols] * buf_ref[t, s:s + rows, :]
    return _silu(acc)


def _in_proj_kernel(x_ref, mod_ref, w_ref, ws_ref, cw_ref, cb_ref, proj_ref, small_ref, h_ref,
                    buf_ref, buf2_ref, *, norm_rows, row_block, conv_block, conv_rows, conv_tiles):
    j = pl.program_id(1)
    tm = x_ref.shape[0]
    tn = w_ref.shape[1]

    assert 0 not in conv_tiles

    @pl.when(j == 0)
    def _():
        shift = mod_ref[0, 0:1, :]
        scale = 1.0 + mod_ref[0, 1:2, :]
        ws = ws_ref[...].astype(BF16)
        for rb in range(0, tm, row_block):
            for r0 in range(rb, rb + row_block, norm_rows):
                x = x_ref[r0:r0 + norm_rows, :]
                hb = ((x * _rms_scale(x)) * scale + shift).astype(BF16)
                h_ref[r0:r0 + norm_rows, :] = hb
                small_ref[r0:r0 + norm_rows, :] = _dot_nt(hb, ws)
            proj_ref[rb:rb + row_block, :] = _dot(h_ref[rb:rb + row_block, :],
                                                  w_ref[...]).astype(BF16)

    is_conv = functools.reduce(jnp.logical_or, [j == t for t in conv_tiles])

    @pl.when(jnp.logical_not(is_conv) & (j != 0))
    def _():
        for rb in range(tm // row_block):
            rows = slice(rb * row_block, (rb + 1) * row_block)
            proj_ref[rows, :] = _dot(h_ref[rows, :], w_ref[...]).astype(BF16)

    @pl.when(is_conv)
    def _():
        nblk = tn // LANES
        bufs = (buf_ref, buf2_ref)
        for t in range(nblk):
            buf_ref[t, 0:HIST, :] = jnp.zeros((HIST, LANES), F32)
        n_blocks = tm // conv_block

        def matmul(ib):
            return _dot(h_ref[ib * conv_block:(ib + 1) * conv_block, :], w_ref[...])

        def stash(ib, res):
            cur, nxt = bufs[ib % 2], bufs[(ib + 1) % 2]
            for t in range(nblk):
                cur[t, HIST:HIST + conv_block, :] = res[:, t * LANES:(t + 1) * LANES]
            for t in range(nblk):
                nxt[t, 0:HIST, :] = res[conv_block - HIST:conv_block, t * LANES:(t + 1) * LANES]

        def epilogue(ib):
            rb = ib * conv_block
            for t in range(nblk):
                for r0 in range(0, conv_block, conv_rows):
                    proj_ref[rb + r0:rb + r0 + conv_rows, t * LANES:(t + 1) * LANES] = _conv_silu(
                        bufs[ib % 2], cw_ref, cb_ref, t, r0, conv_rows).astype(BF16)

        stash(0, matmul(0))
        for ib in range(1, n_blocks):
            res = matmul(ib)
            epilogue(ib - 1)
            stash(ib, res)
        epilogue(n_blocks - 1)


def _in_proj(x2d, mod3, w_main, w_small, conv_w, conv_b, seq, conv_tiles):
    t, d = x2d.shape
    n = w_main.shape[1]
    tm, tn = seq, IN_PROJ_TN
    row_block, conv_block, conv_rows = 512, 512, 256
    return pl.pallas_call(
        functools.partial(_in_proj_kernel, norm_rows=256, row_block=row_block,
                          conv_block=conv_block, conv_rows=conv_rows, conv_tiles=conv_tiles),
        grid=(t // tm, n // tn),
        in_specs=[pl.BlockSpec((tm, d), lambda i, j: (i, 0)),
                  pl.BlockSpec((1, 6, d), lambda i, j: (i, 0, 0)),
                  pl.BlockSpec((d, tn), lambda i, j: (0, j)),
                  pl.BlockSpec((LANES, d), lambda i, j: (0, 0)),
                  pl.BlockSpec((CONV_K, tn), lambda i, j: (0, j)),
                  pl.BlockSpec((1, tn), lambda i, j: (0, j))],
        out_specs=[pl.BlockSpec((tm, tn), lambda i, j: (i, j)),
                   pl.BlockSpec((tm, LANES), lambda i, j: (i, 0))],
        out_shape=[jax.ShapeDtypeStruct((t, n), BF16),
                   jax.ShapeDtypeStruct((t, LANES), F32)],
        scratch_shapes=[pltpu.VMEM((tm, d), BF16),
                        pltpu.VMEM((tn // LANES, HIST + conv_block, LANES), F32),
                        pltpu.VMEM((tn // LANES, HIST + conv_block, LANES), F32)],
        compiler_params=pltpu.CompilerParams(
            dimension_semantics=("parallel", "arbitrary"),
            vmem_limit_bytes=VMEM_LIMIT_BYTES),
        name="in_proj",
    )(x2d, mod3, w_main, w_small, conv_w, conv_b)


def _ssd_kernel(z_ref, xs_ref, bm_ref, cm_ref, sm_ref, dtb_ref, alog_ref, dexp_ref,
                e_ref, out_ref, st_ref, y_s, *, group):
    @pl.when(pl.program_id(1) == 0)
    def _():
        st_ref[...] = jnp.zeros(st_ref.shape, F32)

    nb = z_ref.shape[0]
    L = CHUNK
    G, N, P = SSD_GROUPS, SSD_STATE, SSD_HEADDIM
    d_inner = xs_ref.shape[2]
    gw = d_inner // G
    n_heads = d_inner // P
    tile = SSD_TILE
    heads_per_tile = tile // P
    tril = _tril_mask(L)
    tril_f = tril.astype(F32)
    lane = lax.broadcasted_iota(jnp.int32, (L, LANES), 1)
    lane_row = lax.broadcasted_iota(jnp.int32, (1, tile), 1)
    head_mask = [((lane_row >= q * P) & (lane_row < (q + 1) * P)).astype(BF16)
                 for q in range(heads_per_tile)]
    sq, st = {}, {}
    ssq = {s: jnp.zeros((L, tile), F32) for s in range(nb)}

    def stage0(s):
        dt = jnp.where(lane < n_heads, _softplus(sm_ref[s] + dtb_ref[...]), 0.0)
        a = dt * (-jnp.exp(alog_ref[...]))
        acs = jnp.dot(tril_f, a, precision=HIGHEST, preferred_element_type=F32)
        acs2 = acs * LOG2E
        a_tot = acs[L - 1:L, :]
        w_dt = jnp.exp(a_tot - acs) * dt
        e_tot = jnp.broadcast_to(jnp.exp(a_tot), (SUBLANES, LANES))
        stacked = jnp.concatenate([dt, jnp.exp2(acs2), w_dt, e_tot], axis=0)
        sq[s] = dict(acs2=acs2, acs2_t=acs2.T,
                     s_cat=jnp.concatenate(_split_bf16(stacked), axis=1))

    def stage1(s, g):
        gcols = slice(g * gw, (g + 1) * gw)
        ex = _dot(sq[s]["s_cat"], e_ref[:, gcols])
        xs_g = xs_ref[s, :, gcols].astype(F32)
        bm_g = bm_ref[s, :, g * N:(g + 1) * N]
        cm_g = cm_ref[s, :, g * N:(g + 1) * N]
        st[s, g] = dict(
            eacs_e=ex[L:2 * L], etot_e=ex[3 * L:3 * L + 1], xs_g=xs_g,
            xdt_g=(xs_g * ex[0:L]).astype(BF16), xw_g=(xs_g * ex[2 * L:3 * L]).astype(BF16),
            cb_g=lax.dot_general(cm_g, bm_g, NT_DIMS, preferred_element_type=F32))

    def stage2(s, g):
        e = st[s, g]
        bm_g = bm_ref[s, :, g * N:(g + 1) * N]
        cm_g = cm_ref[s, :, g * N:(g + 1) * N]
        st_g = st_ref[s, g]
        e["y_off"] = _dot(cm_g, st_g.astype(BF16)) * e["eacs_e"]
        st_ref[s, g] = st_g * e["etot_e"] + lax.dot_general(
            bm_g, e["xw_g"], TN_DIMS, preferred_element_type=F32)

    def stage3(s, g):
        e = st[s, g]
        acs2, acs2_t = sq[s]["acs2"], sq[s]["acs2_t"]
        for p in range(gw // tile):
            pc = slice(p * tile, (p + 1) * tile)
            ms = []
            for q in range(heads_per_tile):
                h = (g * gw + p * tile) // P + q
                seg = acs2[:, h:h + 1] - acs2_t[h:h + 1, :]
                dec = jnp.exp2(jnp.where(tril, seg, -jnp.inf))
                ms.append((e["cb_g"] * dec).astype(BF16))
            lhs = jnp.concatenate(ms, axis=1)
            xp = e["xdt_g"][:, pc]
            rhs = jnp.concatenate([xp * m for m in head_mask], axis=0)
            col = slice(g * gw + p * tile, g * gw + (p + 1) * tile)
            y = _dot(lhs, rhs) + e["y_off"][:, pc] + e["xs_g"][:, pc] * dexp_ref[:, col]
            gated = y * _silu(z_ref[s, :, col].astype(F32))
            y_s[s, :, col] = gated
            ssq[s] = ssq[s] + gated * gated

    def stage4(s):
        inv = lax.rsqrt(jnp.sum(ssq[s], axis=-1, keepdims=True) / d_inner + RMS_EPS)
        out_ref[s] = (y_s[s] * inv).astype(BF16)

    for s0 in range(0, nb, group):
        seqs = range(s0, min(s0 + group, nb))
        insts = [(s, g) for s in seqs for g in range(G)]
        for s in seqs:
            stage0(s)
        for stage in (stage1, stage2, stage3):
            for s, g in insts:
                stage(s, g)
        for s in seqs:
            stage4(s)


def _ssd(proj, small, dtb_row, alog_row, dexp_row, expand, d_inner, col_idx):
    bsz, seq, _ = proj.shape
    L, nb = CHUNK, SSD_BATCH
    gn = SSD_GROUPS * SSD_STATE
    blk = lambda width, col: pl.BlockSpec((nb, L, width), lambda b, c: (b, c, col))
    full = lambda shape: pl.BlockSpec(shape, lambda b, c: (0,) * len(shape))
    return pl.pallas_call(
        functools.partial(_ssd_kernel, group=MIX_GROUP),
        grid=(bsz // nb, seq // L),
        in_specs=[blk(d_inner, col_idx["z"]), blk(d_inner, col_idx["xs"]),
                  blk(gn, col_idx["bm"]), blk(gn, col_idx["cm"]), blk(LANES, 0),
                  full((1, LANES)), full((1, LANES)),
                  full((1, d_inner)),
                  full((2 * LANES, d_inner))],
        out_specs=blk(d_inner, 0),
        out_shape=jax.ShapeDtypeStruct((bsz, seq, d_inner), BF16),
        scratch_shapes=[pltpu.VMEM((nb, SSD_GROUPS, SSD_STATE, d_inner // SSD_GROUPS), F32),
                        pltpu.VMEM((nb, L, d_inner), F32)],
        compiler_params=pltpu.CompilerParams(
            dimension_semantics=("parallel", "arbitrary"),
            vmem_limit_bytes=VMEM_LIMIT_BYTES, flags=MIXER_FLAGS),
        name="ssd",
    )(proj, proj, proj, proj, small, dtb_row, alog_row, dexp_row, expand)


def _mlstm_kernel(qk_ref, v_ref, o_ref, sm_ref, ifb_ref, nw_ref, out_ref, c_ref, n_ref, m_ref,
                  *, i_lane, f_lane):
    @pl.when(pl.program_id(1) == 0)
    def _():
        c_ref[...] = jnp.zeros(c_ref.shape, F32)
        n_ref[...] = jnp.zeros(n_ref.shape, F32)
        m_ref[...] = jnp.zeros(m_ref.shape, F32)

    nb = qk_ref.shape[0]
    L = CHUNK
    H = ML_HEADS
    d_qk = qk_ref.shape[2] // 2
    d_v = v_ref.shape[2]
    dk, dv = d_qk // H, d_v // H
    q_scale = dk ** -0.5
    assert 2.0 ** round(np.log2(q_scale)) == q_scale
    tril = _tril_mask(L)
    lane = lax.broadcasted_iota(jnp.int32, (L, LANES), 1)
    seq_gates, st = {}, {}

    def stage0(s):
        gates = sm_ref[s] + ifb_ref[...]
        b_cum = _cumsum_rows(-_softplus(-gates))
        mixed = jnp.where(lane < f_lane, gates, b_cum)
        seq_gates[s] = (mixed, mixed.T)

    def stage1(s, h):
        mixed, mixed_t = seq_gates[s]
        li_col = mixed[:, i_lane + h:i_lane + h + 1]
        li_row = mixed_t[i_lane + h:i_lane + h + 1, :]
        b_col = mixed[:, f_lane + h:f_lane + h + 1]
        b_row = mixed_t[f_lane + h:f_lane + h + 1, :]
        g_tot = b_col[L - 1:L, :]
        m_prev = m_ref[s, h, 0:1, 0:1]
        d = jnp.where(tril, (b_col - b_row) + li_row, -jnp.inf)
        m_inter = b_col + m_prev
        m_t = jnp.maximum(m_inter, jnp.max(d, axis=-1, keepdims=True))
        a_loc = (g_tot - b_col) + li_col
        m_loc = jnp.max(a_loc, axis=0, keepdims=True)
        m_new = jnp.maximum(g_tot + m_prev, m_loc)
        st[s, h] = dict(d=d, m_inter=m_inter, m_t=m_t, a_loc=a_loc, m_new=m_new,
                        sp=jnp.exp(g_tot + m_prev - m_new))

    def stage2(s, h):
        e = st[s, h]
        qb = qk_ref[s, :, h * dk:(h + 1) * dk] * q_scale
        kb = qk_ref[s, :, d_qk + h * dk:d_qk + (h + 1) * dk]
        e["scores"] = (lax.dot_general(qb, kb, NT_DIMS, preferred_element_type=F32)
                       * jnp.exp(e["d"] - e["m_t"]))
        e["w_inter"] = jnp.exp(e["m_inter"] - e["m_t"])

    def stage3(s, h):
        e = st[s, h]
        q = (qk_ref[s, :, h * dk:(h + 1) * dk] * q_scale).astype(F32)
        vb = v_ref[s, :, h * dv:(h + 1) * dv]
        lhs = jnp.concatenate([e["scores"].astype(BF16), (q * e["w_inter"]).astype(BF16)], axis=1)
        rhs = jnp.concatenate([vb, c_ref[s, h].astype(BF16)], axis=0)
        e["num"] = _dot(lhs, rhs)
        den = (jnp.sum(e["scores"], axis=-1, keepdims=True)
               + e["w_inter"] * jnp.sum(q * n_ref[s, h, 0:1, :], axis=-1, keepdims=True))
        e["den"] = jnp.maximum(jnp.abs(den), jnp.exp(-e["m_t"]))

    def stage4(s, h):
        e = st[s, h]
        k = qk_ref[s, :, d_qk + h * dk:d_qk + (h + 1) * dk].astype(F32)
        vb = v_ref[s, :, h * dv:(h + 1) * dv]
        kw = k * jnp.exp(e["a_loc"] - e["m_new"])
        c_ref[s, h] = e["sp"] * c_ref[s, h] + lax.dot_general(
            kw.astype(BF16), vb, TN_DIMS, preferred_element_type=F32)
        n_ref[s, h, 0:1, :] = e["sp"] * n_ref[s, h, 0:1, :] + jnp.sum(kw, axis=0, keepdims=True)
        m_ref[s, h] = jnp.broadcast_to(e["m_new"], m_ref.shape[2:])

    def stage5(s, h):
        e = st[s, h]
        vcols = slice(h * dv, (h + 1) * dv)
        inv_den = 1.0 / e["den"]
        msq = jnp.mean(e["num"] * e["num"], axis=-1, keepdims=True)
        row_scale = inv_den * lax.rsqrt(inv_den * inv_den * msq + RMS_EPS)
        gate = 0.5 * jnp.tanh(0.5 * o_ref[s, :, vcols]) + 0.5
        out_ref[s, :, vcols] = gate * (e["num"] * row_scale * nw_ref[:, vcols]).astype(BF16)

    insts = [(s, h) for s in range(nb) for h in range(H)]
    for s in range(nb):
        stage0(s)
    for stage in (stage1, stage2, stage3, stage4, stage5):
        for inst in insts:
            stage(*inst)


def _mlstm(proj, small, ifb_row, nw_row, d_qk, d_v, col_idx, i_lane, f_lane):
    bsz, seq, _ = proj.shape
    L, nb = CHUNK, MIX_BATCH
    H = ML_HEADS
    blk = lambda width, col: pl.BlockSpec((nb, L, width), lambda b, c: (b, c, col))
    full = lambda shape: pl.BlockSpec(shape, lambda b, c: (0,) * len(shape))
    return pl.pallas_call(
        functools.partial(_mlstm_kernel, i_lane=i_lane, f_lane=f_lane),
        grid=(bsz // nb, seq // L),
        in_specs=[blk(2 * d_qk, col_idx["qk"]), blk(d_v, col_idx["v"]), blk(d_v, col_idx["o"]),
                  blk(LANES, 0), full((1, LANES)), full((1, d_v))],
        out_specs=blk(d_v, 0),
        out_shape=jax.ShapeDtypeStruct((bsz, seq, d_v), BF16),
        scratch_shapes=[pltpu.VMEM((nb, H, d_qk // H, d_v // H), F32),
                        pltpu.VMEM((nb, H, SUBLANES, d_qk // H), F32),
                        pltpu.VMEM((nb, H, SUBLANES, LANES), F32)],
        compiler_params=pltpu.CompilerParams(
            dimension_semantics=("parallel", "arbitrary"),
            vmem_limit_bytes=VMEM_LIMIT_BYTES, flags=MIXER_FLAGS),
        name="mlstm",
    )(proj, proj, proj, small, ifb_row, nw_row)


def _merge_kernel(ys_ref, ym_ref, gs_ref, gm_ref, x_ref, mod_ref, wbs_ref, wbm_ref, wo_ref,
                  out_ref):
    a = _dot(ys_ref[...], wbs_ref[...])
    b = _dot(ym_ref[...], wbm_ref[...])
    merged = (_sigmoid(gs_ref[...].astype(F32)) * a + _sigmoid(gm_ref[...].astype(F32)) * b)
    o = _dot(merged.astype(BF16), wo_ref[...])
    out_ref[...] = x_ref[...] + mod_ref[0, 2:3, :] * o


def _merge(y_ssd, y_ml, proj, x2d, mod3, wbs, wbm, wo, seq, col_idx):
    t, d = x2d.shape
    dv = y_ssd.shape[1]
    tm = 512
    per_b = seq // tm
    const = lambda shape: pl.BlockSpec(shape, lambda i: (0,) * len(shape),
                                       pipeline_mode=pl.Buffered(1))
    return pl.pallas_call(
        _merge_kernel,
        grid=(t // tm,),
        in_specs=[pl.BlockSpec((tm, dv), lambda i: (i, 0)),
                  pl.BlockSpec((tm, dv), lambda i: (i, 0)),
                  pl.BlockSpec((tm, d), lambda i: (i, col_idx["g_ssd"])),
                  pl.BlockSpec((tm, d), lambda i: (i, col_idx["g_ml"])),
                  pl.BlockSpec((tm, d), lambda i: (i, 0)),
                  pl.BlockSpec((1, 6, d), lambda i: (i // per_b, 0, 0)),
                  const((dv, d)), const((dv, d)), const((d, d))],
        out_specs=pl.BlockSpec((tm, d), lambda i: (i, 0)),
        out_shape=jax.ShapeDtypeStruct((t, d), F32),
        compiler_params=pltpu.CompilerParams(
            dimension_semantics=("parallel",),
            vmem_limit_bytes=VMEM_LIMIT_BYTES),
        name="merge",
    )(y_ssd, y_ml, proj, proj, x2d, mod3, wbs, wbm, wo)


def _ffn_kernel(x_ref, mod_ref, wi_ref, wo_ref, fw_ref, out_ref, h_s, acc_s, *, ff_chunk):
    tm = x_ref.shape[0]
    shift = mod_ref[0, 3:4, :]
    scale = 1.0 + mod_ref[0, 4:5, :]
    gate = mod_ref[0, 5:6, :]
    d_ff = wo_ref.shape[0]
    n_chunks = d_ff // ff_chunk
    halves = (slice(0, tm // 2), slice(tm // 2, tm))

    def prologue(r):
        x1 = x_ref[r, :]
        h_s[r, :] = ((x1 * _rms_scale(x1)) * scale + shift).astype(BF16)

    def chunk(r, c):
        cc = slice(c * ff_chunk, (c + 1) * ff_chunk)
        gt = _dot(h_s[r, :], wi_ref[:, cc])
        up = _dot(h_s[r, :], wi_ref[:, d_ff + c * ff_chunk:d_ff + (c + 1) * ff_chunk])
        act = (_silu(gt) * up).astype(BF16)
        part = _dot(act, wo_ref[cc, :])
        if c == 0:
            acc_s[r, :] = part
        else:
            acc_s[r, :] += part

    def epilogue(r):
        x2 = x_ref[r, :] + gate * acc_s[r, :]
        out_ref[r, :] = (x2 * _rms_scale(x2)) * fw_ref[...]

    a, b = halves
    prologue(a)
    chunk(a, 0)
    prologue(b)
    for c in range(n_chunks):
        if c == n_chunks - 1:
            epilogue(a)
        chunk(b, c)
        if c + 1 < n_chunks:
            chunk(a, c + 1)
    epilogue(b)


def _ffn(x1, mod3, wi, wo, fw_row, seq, ff_chunk):
    t, d = x1.shape
    tm = 1024
    per_b = seq // tm
    const = lambda shape: pl.BlockSpec(shape, lambda i: (0,) * len(shape),
                                       pipeline_mode=pl.Buffered(1))
    return pl.pallas_call(
        functools.partial(_ffn_kernel, ff_chunk=ff_chunk),
        grid=(t // tm,),
        in_specs=[pl.BlockSpec((tm, d), lambda i: (i, 0)),
                  pl.BlockSpec((1, 6, d), lambda i: (i // per_b, 0, 0)),
                  const(wi.shape), const(wo.shape), const((1, d))],
        out_specs=pl.BlockSpec((tm, d), lambda i: (i, 0)),
        out_shape=jax.ShapeDtypeStruct((t, d), F32),
        scratch_shapes=[pltpu.VMEM((tm, d), BF16), pltpu.VMEM((tm, d), F32)],
        compiler_params=pltpu.CompilerParams(
            dimension_semantics=("parallel",),
            vmem_limit_bytes=VMEM_LIMIT_BYTES),
        name="ffn",
    )(x1, mod3, wi, wo, fw_row)


def _pad_lanes(row, offset=0):
    return jnp.zeros((1, LANES), F32).at[0, offset:offset + row.shape[0]].set(row)


def kernel(x, c, w_ada, b_ada, w_in, ssd_conv_w, ssd_conv_b, ssd_dt_bias, ssd_a_log, ssd_d,
           ssd_norm_w, mlstm_conv_w, mlstm_conv_b, mlstm_if_bias, mlstm_norm_w,
           w_branch_ssd, w_branch_mlstm, w_out, w_ffn_in, w_ffn_out, final_norm_w):
    bsz, seq, d = x.shape
    depth = w_ada.shape[0]
    d_inner = ssd_norm_w.shape[-1]
    n_heads = ssd_dt_bias.shape[-1]
    gn = SSD_GROUPS * SSD_STATE
    d_qk = mlstm_conv_w.shape[-1] // 2
    d_v = mlstm_norm_w.shape[-1]
    d_ff = w_ffn_out.shape[1]
    assert ssd_conv_w.shape[-1] == d_inner + 2 * gn and d_inner == n_heads * SSD_HEADDIM
    assert n_heads + 2 * ML_HEADS <= LANES and seq % CHUNK == 0 and bsz % MIX_BATCH == 0
    assert d_inner == 2 * d and d_v == 2 * d and 2 * d_qk == 2 * d and gn * 2 == d

    sizes = (d_inner, d_inner + 2 * gn, n_heads, 2 * d_qk, d_v, d_v, 2 * ML_HEADS, 2 * d)
    offs = np.concatenate([[0], np.cumsum(sizes)])
    assert w_in.shape[-1] == offs[-1]
    col_idx = {"z": 0, "xs": 1, "qk": 2, "v": 3, "o": 4,
               "g_ssd": 10, "g_ml": 11, "bm": 24, "cm": 25}
    i_lane, f_lane = n_heads, n_heads + ML_HEADS
    ff_chunk = 256

    expand = (jnp.arange(2 * LANES)[:, None] % LANES
              == (jnp.arange(d_inner)[None, :] // SSD_HEADDIM)).astype(BF16)

    x2d = x.reshape(bsz * seq, d)
    for l in range(depth):
        w_t = jnp.swapaxes(w_in[l], 0, 1)
        tn = IN_PROJ_TN
        pieces = ((offs[0], offs[1] + d_inner),
                  (offs[3], offs[6]),
                  (offs[7], offs[8]),
                  (offs[1] + d_inner, offs[2]))
        assert all((b - a) % tn == 0 for a, b in pieces)
        w_main = _regroup_rows(w_t, [int(c) for a, b in pieces for c in range(a, b, tn)], tn)
        w_small = jnp.concatenate(
            [w_t[offs[2]:offs[3]], w_t[offs[6]:offs[7]],
             jnp.zeros((LANES - n_heads - 2 * ML_HEADS, d), F32)], axis=0)
        wi = w_ffn_in[l].astype(BF16)
        n_main = w_main.shape[1]
        conv_cols = ((d_inner, ssd_conv_w[l][:, :d_inner], ssd_conv_b[l][:d_inner]),
                     (2 * d_inner, mlstm_conv_w[l], mlstm_conv_b[l]),
                     (n_main - 2 * gn, ssd_conv_w[l][:, d_inner:], ssd_conv_b[l][d_inner:]))
        conv_w = jnp.zeros((CONV_K, n_main), F32)
        conv_b = jnp.zeros((1, n_main), F32)
        for off, cw, cb in conv_cols:
            conv_w = conv_w.at[:, off:off + cw.shape[1]].set(cw)
            conv_b = conv_b.at[0, off:off + cw.shape[1]].set(cb)
        conv_tiles = tuple(sorted({t for off, cw, _ in conv_cols
                                   for t in range(off // tn, (off + cw.shape[1] - 1) // tn + 1)}))

        mod3 = _adaln(c, w_ada[l], b_ada[l]).reshape(bsz, 6, d)
        proj, small = _in_proj(x2d, mod3, w_main, w_small, conv_w, conv_b, seq, conv_tiles)
        proj3 = proj.reshape(bsz, seq, n_main)
        small3 = small.reshape(bsz, seq, LANES)
        y_ssd = _ssd(proj3, small3, _pad_lanes(ssd_dt_bias[l]), _pad_lanes(ssd_a_log[l]),
                     jnp.repeat(ssd_d[l], SSD_HEADDIM).reshape(1, d_inner), expand, d_inner,
                     col_idx).reshape(bsz * seq, d_inner)
        y_ml = _mlstm(proj3, small3, _pad_lanes(mlstm_if_bias[l], i_lane),
                      mlstm_norm_w[l].reshape(1, d_v), d_qk, d_v, col_idx, i_lane, f_lane
                      ).reshape(bsz * seq, d_v)
        wbs = (ssd_norm_w[l][:, None] * w_branch_ssd[l]).astype(BF16)
        x1 = _merge(y_ssd, y_ml, proj, x2d, mod3, wbs, w_branch_mlstm[l].astype(BF16),
                    w_out[l].astype(BF16), seq, col_idx)
        assert l == depth - 1 == 0
        x2d = _ffn(x1, mod3, wi, w_ffn_out[l].astype(BF16), final_norm_w.reshape(1, d), seq,
                   ff_chunk)
    return x2d.reshape(bsz, seq, d)
```

```python
import functools

import numpy as np
import jax
import jax.numpy as jnp
from jax import lax
from jax.experimental import pallas as pl
from jax.experimental.pallas import tpu as pltpu

F32 = jnp.float32
BF16 = jnp.bfloat16
HIGHEST = lax.Precision.HIGHEST

LANES = 128
SUBLANES = 8
VMEM_LIMIT_BYTES = 56 * 2**20

RMS_EPS = 1e-6
LOG2E = 1.4426950408889634
CHUNK = 128
CONV_K = 4
HIST = SUBLANES
IN_PROJ_TN = 1024
MIX_BATCH = 4
SSD_BATCH = MIX_BATCH
SSD_TILE = 128
MIX_GROUP = MIX_BATCH
MIXER_FLAGS = None
SSD_GROUPS = 4
SSD_STATE = 128
SSD_HEADDIM = 64
ML_HEADS = 4

NT_DIMS = (((1,), (1,)), ((), ()))
TN_DIMS = (((0,), (0,)), ((), ()))


def _dot(a, b):
    return jnp.dot(a, b, preferred_element_type=F32)


def _dot_nt(a, b):
    return lax.dot_general(a, b, NT_DIMS, preferred_element_type=F32)


def _sigmoid(x):
    return jax.nn.sigmoid(x)


def _silu(x):
    return x * _sigmoid(x)


def _softplus(x):
    return jnp.maximum(x, 0.0) + jnp.log1p(jnp.exp(-jnp.abs(x)))


def _rms_scale(x):
    return lax.rsqrt(jnp.mean(x * x, axis=-1, keepdims=True) + RMS_EPS)


def _split_bf16(v):
    hi = v.astype(BF16)
    lo = (v - hi.astype(F32)).astype(BF16)
    return hi, lo


def _tril_mask(n):
    row = lax.broadcasted_iota(jnp.int32, (n, n), 0)
    col = lax.broadcasted_iota(jnp.int32, (n, n), 1)
    return row >= col


def _cumsum_rows(x):
    n = x.shape[0]
    row = lax.broadcasted_iota(jnp.int32, x.shape, 0)
    k = 1
    while k < n:
        x = x + jnp.where(row >= k, pltpu.roll(x, k, 0), 0.0)
        k *= 2
    return x


def _adaln_kernel(c_ref, w_ref, b_ref, o_ref):
    c = c_ref[...]
    o_ref[...] = jnp.dot(_silu(c), w_ref[...], precision=HIGHEST,
                         preferred_element_type=F32) + b_ref[...]


def _adaln(c, w_ada, b_ada):
    bsz, d = c.shape
    n = w_ada.shape[1]
    tn = d
    return pl.pallas_call(
        _adaln_kernel,
        grid=(n // tn,),
        in_specs=[pl.BlockSpec((bsz, d), lambda j: (0, 0)),
                  pl.BlockSpec((d, tn), lambda j: (0, j)),
                  pl.BlockSpec((1, tn), lambda j: (0, j))],
        out_specs=pl.BlockSpec((bsz, tn), lambda j: (0, j)),
        out_shape=jax.ShapeDtypeStruct((bsz, n), F32),
        compiler_params=pltpu.CompilerParams(dimension_semantics=("parallel",)),
        name="adaln",
    )(c, w_ada, b_ada.reshape(1, n))


def _regroup_kernel(starts_ref, w_ref, o_ref):
    o_ref[...] = w_ref[...].T.astype(BF16)


def _regroup_rows(w_t, src_rows, tn):
    d = w_t.shape[1]
    assert all(r % SUBLANES == 0 for r in src_rows)
    starts = np.array([r // SUBLANES for r in src_rows], np.int32)
    grid_spec = pltpu.PrefetchScalarGridSpec(
        num_scalar_prefetch=1, grid=(len(src_rows),),
        in_specs=[pl.BlockSpec((pl.Element(tn), pl.Element(d)),
                               lambda j, st: (st[j] * SUBLANES, 0))],
        out_specs=pl.BlockSpec((d, tn), lambda j, st: (0, j)))
    return pl.pallas_call(
        _regroup_kernel, grid_spec=grid_spec,
        out_shape=jax.ShapeDtypeStruct((d, tn * len(src_rows)), BF16),
        compiler_params=pltpu.CompilerParams(dimension_semantics=("arbitrary",)),
        name="regroup")(jnp.asarray(starts), w_t)


def _conv_silu(buf_ref, cw_ref, cb_ref, t, r0, rows):
    cols = slice(t * LANES, (t + 1) * LANES)
    acc = cb_ref[:, cols]
    for j in range(CONV_K):
        s = HIST - (CONV_K - 1) + j + r0
        acc = acc + cw_ref[j:j + 1, cols] * buf_ref[t, s:s + rows, :]
    return _silu(acc)


def _in_proj_kernel(x_ref, mod_ref, w_ref, ws_ref, cw_ref, cb_ref, proj_ref, small_ref, h_ref,
                    buf_ref, buf2_ref, *, norm_rows, row_block, conv_block, conv_rows, conv_tiles):
    j = pl.program_id(1)
    tm = x_ref.shape[0]
    tn = w_ref.shape[1]

    assert 0 not in conv_tiles

    @pl.when(j == 0)
    def _():
        shift = mod_ref[0, 0:1, :]
        scale = 1.0 + mod_ref[0, 1:2, :]
        ws = ws_ref[...].astype(BF16)
        for rb in range(0, tm, row_block):
            for r0 in range(rb, rb + row_block, norm_rows):
                x = x_ref[r0:r0 + norm_rows, :]
                hb = ((x * _rms_scale(x)) * scale + shift).astype(BF16)
                h_ref[r0:r0 + norm_rows, :] = hb
                small_ref[r0:r0 + norm_rows, :] = _dot_nt(hb, ws)
            proj_ref[rb:rb + row_block, :] = _dot(h_ref[rb:rb + row_block, :],
                                                  w_ref[...]).astype(BF16)

    is_conv = functools.reduce(jnp.logical_or, [j == t for t in conv_tiles])

    @pl.when(jnp.logical_not(is_conv) & (j != 0))
    def _():
        for rb in range(tm // row_block):
            rows = slice(rb * row_block, (rb + 1) * row_block)
            proj_ref[rows, :] = _dot(h_ref[rows, :], w_ref[...]).astype(BF16)

    @pl.when(is_conv)
    def _():
        nblk = tn // LANES
        bufs = (buf_ref, buf2_ref)
        for t in range(nblk):
            buf_ref[t, 0:HIST, :] = jnp.zeros((HIST, LANES), F32)
        n_blocks = tm // conv_block

        def matmul(ib):
            return _dot(h_ref[ib * conv_block:(ib + 1) * conv_block, :], w_ref[...])

        def stash(ib, res):
            cur, nxt = bufs[ib % 2], bufs[(ib + 1) % 2]
            for t in range(nblk):
                cur[t, HIST:HIST + conv_block, :] = res[:, t * LANES:(t + 1) * LANES]
            for t in range(nblk):
                nxt[t, 0:HIST, :] = res[conv_block - HIST:conv_block, t * LANES:(t + 1) * LANES]

        def epilogue(ib):
            rb = ib * conv_block
            for r0 in range(0, conv_block, conv_rows):
                y = jnp.concatenate([_conv_silu(bufs[ib % 2], cw_ref, cb_ref, t, r0, conv_rows)
                                     for t in range(nblk)], axis=1)
                proj_ref[rb + r0:rb + r0 + conv_rows, :] = y.astype(BF16)

        stash(0, matmul(0))
        for ib in range(1, n_blocks):
            res = matmul(ib)
            epilogue(ib - 1)
            stash(ib, res)
        epilogue(n_blocks - 1)


def _in_proj(x2d, mod3, w_main, w_small, conv_w, conv_b, seq, conv_tiles):
    t, d = x2d.shape
    n = w_main.shape[1]
    tm, tn = seq, IN_PROJ_TN
    row_block, conv_block, conv_rows = 512, 512, 128
    return pl.pallas_call(
        functools.partial(_in_proj_kernel, norm_rows=256, row_block=row_block,
                          conv_block=conv_block, conv_rows=conv_rows, conv_tiles=conv_tiles),
        grid=(t // tm, n // tn),
        in_specs=[pl.BlockSpec((tm, d), lambda i, j: (i, 0)),
                  pl.BlockSpec((1, 6, d), lambda i, j: (i, 0, 0)),
                  pl.BlockSpec((d, tn), lambda i, j: (0, j)),
                  pl.BlockSpec((LANES, d), lambda i, j: (0, 0)),
                  pl.BlockSpec((CONV_K, tn), lambda i, j: (0, j)),
                  pl.BlockSpec((1, tn), lambda i, j: (0, j))],
        out_specs=[pl.BlockSpec((tm, tn), lambda i, j: (i, j)),
                   pl.BlockSpec((tm, LANES), lambda i, j: (i, 0))],
        out_shape=[jax.ShapeDtypeStruct((t, n), BF16),
                   jax.ShapeDtypeStruct((t, LANES), F32)],
        scratch_shapes=[pltpu.VMEM((tm, d), BF16),
                        pltpu.VMEM((tn // LANES, HIST + conv_block, LANES), F32),
                        pltpu.VMEM((tn // LANES, HIST + conv_block, LANES), F32)],
        compiler_params=pltpu.CompilerParams(
            dimension_semantics=("parallel", "arbitrary"),
            vmem_limit_bytes=VMEM_LIMIT_BYTES),
        name="in_proj",
    )(x2d, mod3, w_main, w_small, conv_w, conv_b)


def _ssd_kernel(z_ref, xs_ref, bm_ref, cm_ref, sm_ref, dtb_ref, alog_ref, dexp_ref,
                e_ref, out_ref, st_ref, y_s, *, group):
    @pl.when(pl.program_id(1) == 0)
    def _():
        st_ref[...] = jnp.zeros(st_ref.shape, F32)

    nb = z_ref.shape[0]
    L = CHUNK
    G, N, P = SSD_GROUPS, SSD_STATE, SSD_HEADDIM
    d_inner = xs_ref.shape[2]
    gw = d_inner // G
    n_heads = d_inner // P
    tile = SSD_TILE
    heads_per_tile = tile // P
    tril = _tril_mask(L)
    tril_f = tril.astype(F32)
    lane = lax.broadcasted_iota(jnp.int32, (L, LANES), 1)
    lane_row = lax.broadcasted_iota(jnp.int32, (1, tile), 1)
    head_mask = [((lane_row >= q * P) & (lane_row < (q + 1) * P)).astype(BF16)
                 for q in range(heads_per_tile)]
    sq, st = {}, {}
    ssq = {s: jnp.zeros((L, tile), F32) for s in range(nb)}

    def stage0(s):
        dt = jnp.where(lane < n_heads, _softplus(sm_ref[s] + dtb_ref[...]), 0.0)
        a = dt * (-jnp.exp(alog_ref[...]))
        acs = jnp.dot(tril_f, a, precision=HIGHEST, preferred_element_type=F32)
        acs2 = acs * LOG2E
        a_tot = acs[L - 1:L, :]
        w_dt = jnp.exp(a_tot - acs) * dt
        e_tot = jnp.broadcast_to(jnp.exp(a_tot), (SUBLANES, LANES))
        stacked = jnp.concatenate([dt, jnp.exp2(acs2), w_dt, e_tot], axis=0)
        sq[s] = dict(acs2=acs2, acs2_t=acs2.T,
                     s_cat=jnp.concatenate(_split_bf16(stacked), axis=1))

    def stage1(s, g):
        gcols = slice(g * gw, (g + 1) * gw)
        ex = _dot(sq[s]["s_cat"], e_ref[:, gcols])
        xs_g = xs_ref[s, :, gcols].astype(F32)
        bm_g = bm_ref[s, :, g * N:(g + 1) * N]
        cm_g = cm_ref[s, :, g * N:(g + 1) * N]
        st[s, g] = dict(
            eacs_e=ex[L:2 * L], etot_e=ex[3 * L:3 * L + 1], xs_g=xs_g,
            xdt_g=(xs_g * ex[0:L]).astype(BF16), xw_g=(xs_g * ex[2 * L:3 * L]).astype(BF16),
            cb_g=lax.dot_general(cm_g, bm_g, NT_DIMS, preferred_element_type=F32))

    def stage2(s, g):
        e = st[s, g]
        bm_g = bm_ref[s, :, g * N:(g + 1) * N]
        cm_g = cm_ref[s, :, g * N:(g + 1) * N]
        st_g = st_ref[s, g]
        e["y_off"] = _dot(cm_g, st_g.astype(BF16)) * e["eacs_e"]
        st_ref[s, g] = st_g * e["etot_e"] + lax.dot_general(
            bm_g, e["xw_g"], TN_DIMS, preferred_element_type=F32)

    def stage3(s, g):
        e = st[s, g]
        acs2, acs2_t = sq[s]["acs2"], sq[s]["acs2_t"]
        for p in range(gw // tile):
            pc = slice(p * tile, (p + 1) * tile)
            ms = []
            for q in range(heads_per_tile):
                h = (g * gw + p * tile) // P + q
                seg = acs2[:, h:h + 1] - acs2_t[h:h + 1, :]
                dec = jnp.exp2(jnp.where(tril, seg, -jnp.inf))
                ms.append((e["cb_g"] * dec).astype(BF16))
            lhs = jnp.concatenate(ms, axis=1)
            xp = e["xdt_g"][:, pc]
            rhs = jnp.concatenate([xp * m for m in head_mask], axis=0)
            col = slice(g * gw + p * tile, g * gw + (p + 1) * tile)
            y = _dot(lhs, rhs) + e["y_off"][:, pc] + e["xs_g"][:, pc] * dexp_ref[:, col]
            gated = y * _silu(z_ref[s, :, col].astype(F32))
            y_s[s, :, col] = gated
            ssq[s] = ssq[s] + gated * gated

    def stage4(s):
        inv = lax.rsqrt(jnp.sum(ssq[s], axis=-1, keepdims=True) / d_inner + RMS_EPS)
        out_ref[s] = (y_s[s] * inv).astype(BF16)

    for s0 in range(0, nb, group):
        seqs = range(s0, min(s0 + group, nb))
        insts = [(s, g) for g in range(G) for s in seqs]
        for s in seqs:
            stage0(s)
        for stage in (stage1, stage2, stage3):
            for s, g in insts:
                stage(s, g)
        for s in seqs:
            stage4(s)


def _ssd(proj, small, dtb_row, alog_row, dexp_row, expand, d_inner, col_idx):
    bsz, seq, _ = proj.shape
    L, nb = CHUNK, SSD_BATCH
    gn = SSD_GROUPS * SSD_STATE
    blk = lambda width, col: pl.BlockSpec((nb, L, width), lambda b, c: (b, c, col))
    full = lambda shape: pl.BlockSpec(shape, lambda b, c: (0,) * len(shape))
    return pl.pallas_call(
        functools.partial(_ssd_kernel, group=MIX_GROUP),
        grid=(bsz // nb, seq // L),
        in_specs=[blk(d_inner, col_idx["z"]), blk(d_inner, col_idx["xs"]),
                  blk(gn, col_idx["bm"]), blk(gn, col_idx["cm"]), blk(LANES, 0),
                  full((1, LANES)), full((1, LANES)),
                  full((1, d_inner)),
                  full((2 * LANES, d_inner))],
        out_specs=blk(d_inner, 0),
        out_shape=jax.ShapeDtypeStruct((bsz, seq, d_inner), BF16),
        scratch_shapes=[pltpu.VMEM((nb, SSD_GROUPS, SSD_STATE, d_inner // SSD_GROUPS), F32),
                        pltpu.VMEM((nb, L, d_inner), F32)],
        compiler_params=pltpu.CompilerParams(
            dimension_semantics=("parallel", "arbitrary"),
            vmem_limit_bytes=VMEM_LIMIT_BYTES, flags=MIXER_FLAGS),
        name="ssd",
    )(proj, proj, proj, proj, small, dtb_row, alog_row, dexp_row, expand)


def _mlstm_kernel(qk_ref, v_ref, o_ref, sm_ref, ifb_ref, nw_ref, out_ref, c_ref, n_ref, m_ref,
                  *, i_lane, f_lane):
    @pl.when(pl.program_id(1) == 0)
    def _():
        c_ref[...] = jnp.zeros(c_ref.shape, F32)
        n_ref[...] = jnp.zeros(n_ref.shape, F32)
        m_ref[...] = jnp.zeros(m_ref.shape, F32)

    nb = qk_ref.shape[0]
    L = CHUNK
    H = ML_HEADS
    d_qk = qk_ref.shape[2] // 2
    d_v = v_ref.shape[2]
    dk, dv = d_qk // H, d_v // H
    q_scale = dk ** -0.5
    assert 2.0 ** round(np.log2(q_scale)) == q_scale
    tril = _tril_mask(L)
    lane = lax.broadcasted_iota(jnp.int32, (L, LANES), 1)
    seq_gates, st = {}, {}

    def stage0(s):
        gates = sm_ref[s] + ifb_ref[...]
        b_cum = _cumsum_rows(-_softplus(-gates))
        mixed = jnp.where(lane < f_lane, gates, b_cum)
        seq_gates[s] = (mixed, mixed.T)

    def stage1(s, h):
        mixed, mixed_t = seq_gates[s]
        li_col = mixed[:, i_lane + h:i_lane + h + 1]
        li_row = mixed_t[i_lane + h:i_lane + h + 1, :]
        b_col = mixed[:, f_lane + h:f_lane + h + 1]
        b_row = mixed_t[f_lane + h:f_lane + h + 1, :]
        g_tot = b_col[L - 1:L, :]
        m_prev = m_ref[s, h, 0:1, 0:1]
        d = jnp.where(tril, (b_col - b_row) + li_row, -jnp.inf)
        m_inter = b_col + m_prev
        m_t = jnp.maximum(m_inter, jnp.max(d, axis=-1, keepdims=True))
        a_loc = (g_tot - b_col) + li_col
        m_loc = jnp.max(a_loc, axis=0, keepdims=True)
        m_new = jnp.maximum(g_tot + m_prev, m_loc)
        st[s, h] = dict(d=d, m_inter=m_inter, m_t=m_t, a_loc=a_loc, m_new=m_new,
                        sp=jnp.exp(g_tot + m_prev - m_new))

    def stage2(s, h):
        e = st[s, h]
        qb = qk_ref[s, :, h * dk:(h + 1) * dk] * q_scale
        kb = qk_ref[s, :, d_qk + h * dk:d_qk + (h + 1) * dk]
        e["scores"] = (lax.dot_general(qb, kb, NT_DIMS, preferred_element_type=F32)
                       * jnp.exp(e["d"] - e["m_t"]))
        e["w_inter"] = jnp.exp(e["m_inter"] - e["m_t"])

    def stage3(s, h):
        e = st[s, h]
        q = (qk_ref[s, :, h * dk:(h + 1) * dk] * q_scale).astype(F32)
        vb = v_ref[s, :, h * dv:(h + 1) * dv]
        lhs = jnp.concatenate([e["scores"].astype(BF16), (q * e["w_inter"]).astype(BF16)], axis=1)
        rhs = jnp.concatenate([vb, c_ref[s, h].astype(BF16)], axis=0)
        e["num"] = _dot(lhs, rhs)
        den = (jnp.sum(e["scores"], axis=-1, keepdims=True)
               + e["w_inter"] * jnp.sum(q * n_ref[s, h, 0:1, :], axis=-1, keepdims=True))
        e["den"] = jnp.maximum(jnp.abs(den), jnp.exp(-e["m_t"]))

    def stage4(s, h):
        e = st[s, h]
        k = qk_ref[s, :, d_qk + h * dk:d_qk + (h + 1) * dk].astype(F32)
        vb = v_ref[s, :, h * dv:(h + 1) * dv]
        kw = k * jnp.exp(e["a_loc"] - e["m_new"])
        c_ref[s, h] = e["sp"] * c_ref[s, h] + lax.dot_general(
            kw.astype(BF16), vb, TN_DIMS, preferred_element_type=F32)
        n_ref[s, h, 0:1, :] = e["sp"] * n_ref[s, h, 0:1, :] + jnp.sum(kw, axis=0, keepdims=True)
        m_ref[s, h] = jnp.broadcast_to(e["m_new"], m_ref.shape[2:])

    def stage5(s, h):
        e = st[s, h]
        vcols = slice(h * dv, (h + 1) * dv)
        inv_den = 1.0 / e["den"]
        msq = jnp.mean(e["num"] * e["num"], axis=-1, keepdims=True)
        row_scale = inv_den * lax.rsqrt(inv_den * inv_den * msq + RMS_EPS)
        gate = 0.5 * jnp.tanh(0.5 * o_ref[s, :, vcols]) + 0.5
        out_ref[s, :, vcols] = gate * (e["num"] * row_scale * nw_ref[:, vcols]).astype(BF16)

    insts = [(s, h) for s in range(nb) for h in range(H)]
    for s in range(nb):
        stage0(s)
    for stage in (stage1, stage2, stage3, stage4, stage5):
        for inst in insts:
            stage(*inst)


def _mlstm(proj, small, ifb_row, nw_row, d_qk, d_v, col_idx, i_lane, f_lane):
    bsz, seq, _ = proj.shape
    L, nb = CHUNK, MIX_BATCH
    H = ML_HEADS
    blk = lambda width, col: pl.BlockSpec((nb, L, width), lambda b, c: (b, c, col))
    full = lambda shape: pl.BlockSpec(shape, lambda b, c: (0,) * len(shape))
    return pl.pallas_call(
        functools.partial(_mlstm_kernel, i_lane=i_lane, f_lane=f_lane),
        grid=(bsz // nb, seq // L),
        in_specs=[blk(2 * d_qk, col_idx["qk"]), blk(d_v, col_idx["v"]), blk(d_v, col_idx["o"]),
                  blk(LANES, 0), full((1, LANES)), full((1, d_v))],
        out_specs=blk(d_v, 0),
        out_shape=jax.ShapeDtypeStruct((bsz, seq, d_v), BF16),
        scratch_shapes=[pltpu.VMEM((nb, H, d_qk // H, d_v // H), F32),
                        pltpu.VMEM((nb, H, SUBLANES, d_qk // H), F32),
                        pltpu.VMEM((nb, H, SUBLANES, LANES), F32)],
        compiler_params=pltpu.CompilerParams(
            dimension_semantics=("parallel", "arbitrary"),
            vmem_limit_bytes=VMEM_LIMIT_BYTES, flags=MIXER_FLAGS),
        name="mlstm",
    )(proj, proj, proj, small, ifb_row, nw_row)


def _merge_kernel(ys_ref, ym_ref, gs_ref, gm_ref, x_ref, mod_ref, wbs_ref, wbm_ref, wo_ref,
                  out_ref):
    a = _dot(ys_ref[...], wbs_ref[...])
    b = _dot(ym_ref[...], wbm_ref[...])
    merged = (_sigmoid(gs_ref[...].astype(F32)) * a + _sigmoid(gm_ref[...].astype(F32)) * b)
    o = _dot(merged.astype(BF16), wo_ref[...])
    out_ref[...] = x_ref[...] + mod_ref[0, 2:3, :] * o


def _merge(y_ssd, y_ml, proj, x2d, mod3, wbs, wbm, wo, seq, col_idx):
    t, d = x2d.shape
    dv = y_ssd.shape[1]
    tm = 512
    per_b = seq // tm
    const = lambda shape: pl.BlockSpec(shape, lambda i: (0,) * len(shape),
                                       pipeline_mode=pl.Buffered(1))
    return pl.pallas_call(
        _merge_kernel,
        grid=(t // tm,),
        in_specs=[pl.BlockSpec((tm, dv), lambda i: (i, 0)),
                  pl.BlockSpec((tm, dv), lambda i: (i, 0)),
                  pl.BlockSpec((tm, d), lambda i: (i, col_idx["g_ssd"])),
                  pl.BlockSpec((tm, d), lambda i: (i, col_idx["g_ml"])),
                  pl.BlockSpec((tm, d), lambda i: (i, 0)),
                  pl.BlockSpec((1, 6, d), lambda i: (i // per_b, 0, 0)),
                  const((dv, d)), const((dv, d)), const((d, d))],
        out_specs=pl.BlockSpec((tm, d), lambda i: (i, 0)),
        out_shape=jax.ShapeDtypeStruct((t, d), F32),
        compiler_params=pltpu.CompilerParams(
            dimension_semantics=("parallel",),
            vmem_limit_bytes=VMEM_LIMIT_BYTES),
        name="merge",
    )(y_ssd, y_ml, proj, proj, x2d, mod3, wbs, wbm, wo)


def _ffn_kernel(x_ref, mod_ref, wi_ref, wo_ref, fw_ref, out_ref, h_s, acc_s, *, ff_chunk):
    tm = x_ref.shape[0]
    shift = mod_ref[0, 3:4, :]
    scale = 1.0 + mod_ref[0, 4:5, :]
    gate = mod_ref[0, 5:6, :]
    d_ff = wo_ref.shape[0]
    n_chunks = d_ff // ff_chunk
    halves = (slice(0, tm // 2), slice(tm // 2, tm))

    def prologue(r):
        x1 = x_ref[r, :]
        h_s[r, :] = ((x1 * _rms_scale(x1)) * scale + shift).astype(BF16)

    def chunk(r, c):
        cc = slice(c * ff_chunk, (c + 1) * ff_chunk)
        gt = _dot(h_s[r, :], wi_ref[:, cc])
        up = _dot(h_s[r, :], wi_ref[:, d_ff + c * ff_chunk:d_ff + (c + 1) * ff_chunk])
        act = (_silu(gt) * up).astype(BF16)
        part = _dot(act, wo_ref[cc, :])
        if c == 0:
            acc_s[r, :] = part
        else:
            acc_s[r, :] += part

    def epilogue(r):
        x2 = x_ref[r, :] + gate * acc_s[r, :]
        out_ref[r, :] = (x2 * _rms_scale(x2)) * fw_ref[...]

    a, b = halves
    prologue(a)
    chunk(a, 0)
    prologue(b)
    for c in range(n_chunks):
        if c == n_chunks - 1:
            epilogue(a)
        chunk(b, c)
        if c + 1 < n_chunks:
            chunk(a, c + 1)
    epilogue(b)


def _ffn(x1, mod3, wi, wo, fw_row, seq, ff_chunk):
    t, d = x1.shape
    tm = 1024
    per_b = seq // tm
    const = lambda shape: pl.BlockSpec(shape, lambda i: (0,) * len(shape),
                                       pipeline_mode=pl.Buffered(1))
    return pl.pallas_call(
        functools.partial(_ffn_kernel, ff_chunk=ff_chunk),
        grid=(t // tm,),
        in_specs=[pl.BlockSpec((tm, d), lambda i: (i, 0)),
                  pl.BlockSpec((1, 6, d), lambda i: (i // per_b, 0, 0)),
                  const(wi.shape), const(wo.shape), const((1, d))],
        out_specs=pl.BlockSpec((tm, d), lambda i: (i, 0)),
        out_shape=jax.ShapeDtypeStruct((t, d), F32),
        scratch_shapes=[pltpu.VMEM((tm, d), BF16), pltpu.VMEM((tm, d), F32)],
        compiler_params=pltpu.CompilerParams(
            dimension_semantics=("parallel",),
            vmem_limit_bytes=VMEM_LIMIT_BYTES),
        name="ffn",
    )(x1, mod3, wi, wo, fw_row)


def _pad_lanes(row, offset=0):
    return jnp.zeros((1, LANES), F32).at[0, offset:offset + row.shape[0]].set(row)


def kernel(x, c, w_ada, b_ada, w_in, ssd_conv_w, ssd_conv_b, ssd_dt_bias, ssd_a_log, ssd_d,
           ssd_norm_w, mlstm_conv_w, mlstm_conv_b, mlstm_if_bias, mlstm_norm_w,
           w_branch_ssd, w_branch_mlstm, w_out, w_ffn_in, w_ffn_out, final_norm_w):
    bsz, seq, d = x.shape
    depth = w_ada.shape[0]
    d_inner = ssd_norm_w.shape[-1]
    n_heads = ssd_dt_bias.shape[-1]
    gn = SSD_GROUPS * SSD_STATE
    d_qk = mlstm_conv_w.shape[-1] // 2
    d_v = mlstm_norm_w.shape[-1]
    d_ff = w_ffn_out.shape[1]
    assert ssd_conv_w.shape[-1] == d_inner + 2 * gn and d_inner == n_heads * SSD_HEADDIM
    assert n_heads + 2 * ML_HEADS <= LANES and seq % CHUNK == 0 and bsz % MIX_BATCH == 0
    assert d_inner == 2 * d and d_v == 2 * d and 2 * d_qk == 2 * d and gn * 2 == d

    sizes = (d_inner, d_inner + 2 * gn, n_heads, 2 * d_qk, d_v, d_v, 2 * ML_HEADS, 2 * d)
    offs = np.concatenate([[0], np.cumsum(sizes)])
    assert w_in.shape[-1] == offs[-1]
    col_idx = {"z": 0, "xs": 1, "qk": 2, "v": 3, "o": 4,
               "g_ssd": 10, "g_ml": 11, "bm": 24, "cm": 25}
    i_lane, f_lane = n_heads, n_heads + ML_HEADS
    ff_chunk = 256

    expand = (jnp.arange(2 * LANES)[:, None] % LANES
              == (jnp.arange(d_inner)[None, :] // SSD_HEADDIM)).astype(BF16)

    x2d = x.reshape(bsz * seq, d)
    for l in range(depth):
        w_t = jnp.swapaxes(w_in[l], 0, 1)
        tn = IN_PROJ_TN
        pieces = ((offs[0], offs[1] + d_inner),
                  (offs[3], offs[6]),
                  (offs[7], offs[8]),
                  (offs[1] + d_inner, offs[2]))
        assert all((b - a) % tn == 0 for a, b in pieces)
        w_main = _regroup_rows(w_t, [int(c) for a, b in pieces for c in range(a, b, tn)], tn)
        w_small = jnp.concatenate(
            [w_t[offs[2]:offs[3]], w_t[offs[6]:offs[7]],
             jnp.zeros((LANES - n_heads - 2 * ML_HEADS, d), F32)], axis=0)
        wi = w_ffn_in[l].astype(BF16)
        n_main = w_main.shape[1]
        conv_cols = ((d_inner, ssd_conv_w[l][:, :d_inner], ssd_conv_b[l][:d_inner]),
                     (2 * d_inner, mlstm_conv_w[l], mlstm_conv_b[l]),
                     (n_main - 2 * gn, ssd_conv_w[l][:, d_inner:], ssd_conv_b[l][d_inner:]))
        conv_w = jnp.zeros((CONV_K, n_main), F32)
        conv_b = jnp.zeros((1, n_main), F32)
        for off, cw, cb in conv_cols:
            conv_w = conv_w.at[:, off:off + cw.shape[1]].set(cw)
            conv_b = conv_b.at[0, off:off + cw.shape[1]].set(cb)
        conv_tiles = tuple(sorted({t for off, cw, _ in conv_cols
                                   for t in range(off // tn, (off + cw.shape[1] - 1) // tn + 1)}))

        mod3 = _adaln(c, w_ada[l], b_ada[l]).reshape(bsz, 6, d)
        proj, small = _in_proj(x2d, mod3, w_main, w_small, conv_w, conv_b, seq, conv_tiles)
        proj3 = proj.reshape(bsz, seq, n_main)
        small3 = small.reshape(bsz, seq, LANES)
        y_ssd = _ssd(proj3, small3, _pad_lanes(ssd_dt_bias[l]), _pad_lanes(ssd_a_log[l]),
                     jnp.repeat(ssd_d[l], SSD_HEADDIM).reshape(1, d_inner), expand, d_inner,
                     col_idx).reshape(bsz * seq, d_inner)
        y_ml = _mlstm(proj3, small3, _pad_lanes(mlstm_if_bias[l], i_lane),
                      mlstm_norm_w[l].reshape(1, d_v), d_qk, d_v, col_idx, i_lane, f_lane
                      ).reshape(bsz * seq, d_v)
        wbs = (ssd_norm_w[l][:, None] * w_branch_ssd[l]).astype(BF16)
        x1 = _merge(y_ssd, y_ml, proj, x2d, mod3, wbs, w_branch_mlstm[l].astype(BF16),
                    w_out[l].astype(BF16), seq, col_idx)
        assert l == depth - 1 == 0
        x2d = _ffn(x1, mod3, wi, w_ffn_out[l].astype(BF16), final_norm_w.reshape(1, d), seq,
                   ff_chunk)
    return x2d.reshape(bsz, seq, d)
```
